```python
import math
import jax
import jax.numpy as jnp
from jax import lax
import numpy as np

D_MODEL = 4096
BATCH = 16
SEQ = 256
DEPTH = 2
DEC_BATCH = 4
DEC_SEQ = 4096
PAST_LEN = 512

GRID_W = 64
N_HEADS_A = 16
HEAD_DIM_A = 64
ATTN_W = N_HEADS_A * 2 * HEAD_DIM_A
AXIS_DIM = HEAD_DIM_A // 2
ROPE_BASE = 10000.0
Q_BLOCK = 128
CONV_W = 1024
CONV_K = 31
N_HEADS_R = 8
HEAD_DIM_R = 128
REC_W = N_HEADS_R * HEAD_DIM_R
CHUNK = 32
N_EXPERTS = 16
EXPERT_FF = 2048
CAP_FACTOR = 2
N_BRANCH = 3
N_MOD = 6
EPS = 1e-6
PROJ_SPLITS = (ATTN_W, ATTN_W, ATTN_W, 2 * CONV_W, REC_W, REC_W, REC_W, REC_W, REC_W, N_BRANCH * D_MODEL)
PROJ_W = 3 * ATTN_W + 2 * CONV_W + 5 * REC_W + N_BRANCH * D_MODEL

kernel_name = 'hybrid_diffattn_conformer_hgrn2_ecmoe_prefix_dit'


def _rms_norm(x, g):
    xf = x.astype(jnp.float32)
    y = xf * lax.rsqrt(jnp.mean(xf * xf, axis=-1, keepdims=True) + EPS)
    return (y * g.astype(jnp.float32)).astype(x.dtype)


def _layer_norm(x, g, b):
    xf = x.astype(jnp.float32)
    mu = jnp.mean(xf, axis=-1, keepdims=True)
    var = jnp.mean(jnp.square(xf - mu), axis=-1, keepdims=True)
    y = (xf - mu) * lax.rsqrt(var + EPS) * g.astype(jnp.float32) + b.astype(jnp.float32)
    return y.astype(x.dtype)


def _ada(cond, w, b):
    m = jax.nn.silu(cond) @ w + b
    return jnp.split(m[:, None, :], N_MOD, axis=-1)


def _modulate(x, g, shift, scale):
    return _rms_norm(x, g) * (1.0 + scale) + shift


def _split_proj(z):
    parts, off = [], 0
    for w in PROJ_SPLITS:
        parts.append(z[..., off:off + w])
        off += w
    return parts


def _axial_rope(n_tok, dtype):
    rows = n_tok // GRID_W
    t = jnp.arange(rows * GRID_W)
    row = (t // GRID_W).astype(jnp.float32)
    col = (t % GRID_W).astype(jnp.float32)
    inv = ROPE_BASE ** (-jnp.arange(0, AXIS_DIM, 2, dtype=jnp.float32) / AXIS_DIM)
    ar = row[:, None] * inv
    ac = col[:, None] * inv
    ang = jnp.concatenate([ar, ar, ac, ac], axis=-1)
    return jnp.cos(ang).astype(dtype), jnp.sin(ang).astype(dtype)


def _rot_half(z):
    z1, z2 = jnp.split(z, 2, axis=-1)
    return jnp.concatenate([-z2, z1], axis=-1)


def _apply_rope(x, cos, sin):
    xr, xc = jnp.split(x, 2, axis=-1)
    rot = jnp.concatenate([_rot_half(xr), _rot_half(xc)], axis=-1)
    return x * cos[:, None, None, :] + rot * sin[:, None, None, :]


def _qkv(zq, zk, zv, qk_g):
    B, L = zq.shape[:2]
    q = _rms_norm(zq.reshape(B, L, N_HEADS_A, 2, HEAD_DIM_A), qk_g[0])
    k = _rms_norm(zk.reshape(B, L, N_HEADS_A, 2, HEAD_DIM_A), qk_g[1])
    v = zv.reshape(B, L, N_HEADS_A, 2 * HEAD_DIM_A)
    return q, k, v


def _diff_attention(q, k, v, lam):
    B, Lq = q.shape[:2]
    nb = Lq // Q_BLOCK
    qb = jnp.swapaxes(q.reshape(B, nb, Q_BLOCK, N_HEADS_A, 2, HEAD_DIM_A), 0, 1)
    scale = HEAD_DIM_A ** -0.5

    def block(qi):
        s = jnp.einsum('bqhmd,bkhmd->bhmqk', qi, k).astype(jnp.float32) * scale
        p = jax.nn.softmax(s, axis=-1)
        a = p[:, :, 0] - lam * p[:, :, 1]
        return jnp.einsum('bhqk,bkhe->bqhe', a.astype(v.dtype), v)

    out = lax.map(block, qb)
    return jnp.swapaxes(out, 0, 1).reshape(B, Lq, N_HEADS_A, 2 * HEAD_DIM_A)


def _conv_branch(z_glu, cw, cb, lng, lnb, w_o):
    a, b = jnp.split(z_glu, 2, axis=-1)
    u = a * jax.nn.sigmoid(b)
    u = lax.conv_general_dilated(u, cw[:, None, :], window_strides=(1,),
                                 padding=[(CONV_K // 2, CONV_K // 2)],
                                 dimension_numbers=('NWC', 'WIO', 'NWC'),
                                 feature_group_count=CONV_W) + cb
    u = jax.nn.silu(_layer_norm(u, lng, lnb))
    return u @ w_o


def _chunk_scan(q, k, v, logf, s0):
    G, B, L, H, _ = q.shape
    dv = v.shape[-1]
    nc = L // CHUNK

    def chunks(z):
        z = z.astype(jnp.float32).reshape(G, B, nc, CHUNK, H, z.shape[-1])
        return z.transpose(2, 0, 1, 4, 3, 5)

    mask = jnp.tril(jnp.ones((CHUNK, CHUNK), dtype=bool))[:, :, None]

    def step(S, inp):
        qc, kc, vc, gc = inp
        b = jnp.cumsum(gc, axis=-2)
        rel = jnp.where(mask, b[..., :, None, :] - b[..., None, :, :], -jnp.inf)
        att = jnp.einsum('gbhtk,gbhsk,gbhtsk->gbhts', qc, kc, jnp.exp(rel))
        o = (jnp.einsum('gbhts,gbhsv->gbhtv', att, vc)
             + jnp.einsum('gbhtk,gbhkv->gbhtv', qc * jnp.exp(b), S))
        b_end = b[..., -1:, :]
        S = (jnp.exp(b_end)[..., 0, :, None] * S
             + jnp.einsum('gbhsk,gbhsv->gbhkv', kc * jnp.exp(b_end - b), vc))
        return S, o

    s_fin, o = lax.scan(step, s0, (chunks(q), chunks(k), chunks(v), chunks(logf)))
    o = o.transpose(1, 2, 0, 4, 3, 5).reshape(G, B, L, H, dv)
    return o, s_fin


def _rec_branch(zq, zi, zff, zfb, zg, lb, s0, norm_g, w_o):
    B, L = zq.shape[:2]
    hs = lambda z: z.reshape(z.shape[:-1] + (N_HEADS_R, HEAD_DIM_R))
    rev = lambda z: z[:, ::-1]
    q = hs(jax.nn.silu(zq))
    v = hs(zi)
    lbb = lb[:, None, None, :]
    f = hs(lbb + (1.0 - lbb) * jax.nn.sigmoid(jnp.stack([zff, zfb]).astype(jnp.float32)))
    k = 1.0 - f
    logf = jnp.log(f)
    q2 = jnp.stack([q, rev(q)])
    v2 = jnp.stack([v, rev(v)])
    k2 = jnp.stack([k[0], rev(k[1])])
    g2 = jnp.stack([logf[0], rev(logf[1])])
    o2, s_fin = _chunk_scan(q2, k2, v2, g2, s0)
    o = (o2[0] + rev(o2[1])).astype(zq.dtype)
    o = _rms_norm(o, norm_g) * jax.nn.silu(hs(zg))
    return o.reshape(B, L, REC_W) @ w_o, s_fin


def _expert_choice_ffn(h, w_router, w_gate, w_up, w_down):
    B, n, D = h.shape
    cap = CAP_FACTOR * n // N_EXPERTS
    aff = jax.nn.softmax((h @ w_router).astype(jnp.float32), axis=-1)
    gate, idx = lax.top_k(jnp.swapaxes(aff, 1, 2), cap)
    xs = jax.vmap(lambda hb, ib: hb[ib])(h, idx)
    hid = jax.nn.silu(jnp.einsum('becd,edf->becf', xs, w_gate)) * jnp.einsum('becd,edf->becf', xs, w_up)
    ys = jnp.einsum('becf,efd->becd', hid, w_down) * gate[..., None].astype(h.dtype)
    return jax.vmap(lambda yb, ib: jnp.zeros((n, D), yb.dtype).at[ib.reshape(-1)].add(yb.reshape(-1, D)))(ys, idx)


def _layer(x, cond, lp, lb, lam_val, lam_init, s0, rope=None, ctx_kv=None):
    (n1, n2, wa, ba, wi, qkg, ang, wao, cw, cb, clg, clb, wco,
     hng, who, wo, wr, weg, weu, wed) = lp
    sh1, sc1, g1, sh2, sc2, g2 = _ada(cond, wa, ba)
    h = _modulate(x, n1, sh1, sc1)
    zq, zk, zv, zglu, zrq, zri, zff, zfb, zrg, zgate = _split_proj(h @ wi)
    q, k, v = _qkv(zq, zk, zv, qkg)
    if rope is not None:
        q = _apply_rope(q, rope[0], rope[1])
        k = _apply_rope(k, rope[0], rope[1])
    kk, vv = k, v
    if ctx_kv is not None:
        kk = jnp.concatenate([ctx_kv[0], k], axis=1)
        vv = jnp.concatenate([ctx_kv[1], v], axis=1)
    o = _diff_attention(q, kk, vv, lam_val)
    B, L = x.shape[:2]
    ya = (_rms_norm(o, ang) * (1.0 - lam_init)).reshape(B, L, ATTN_W) @ wao
    yc = _conv_branch(zglu, cw, cb, clg, clb, wco)
    yh, s_fin = _rec_branch(zrq, zri, zff, zfb, zrg, lb, s0, hng, who)
    ga, gc, gh = jnp.split(zgate, N_BRANCH, axis=-1)
    m = jax.nn.sigmoid(ga) * ya + jax.nn.sigmoid(gc) * yc + jax.nn.sigmoid(gh) * yh
    x = x + g1 * (m @ wo)
    h2 = _modulate(x, n2, sh2, sc2)
    x = x + g2 * _expert_choice_ffn(h2, wr, weg, weu, wed)
    return x, k, v, s_fin


def setup_inputs(seed: int = 0) -> dict:
    key = jax.random.key(seed)
    ks = jax.random.split(key, 29)
    nrm = lambda k, shape, s: jax.random.normal(k, shape, jnp.float32) * s
    D = D_MODEL
    return {
        'x_prompt': nrm(ks[0], (BATCH, SEQ, D), 1.0),
        'x_sample': nrm(ks[1], (DEC_BATCH, DEC_SEQ, D), 1.0),
        'cache_k': nrm(ks[2], (DEC_BATCH, DEPTH, PAST_LEN, N_HEADS_A, 2, HEAD_DIM_A), 1.0),
        'cache_v': nrm(ks[3], (DEC_BATCH, DEPTH, PAST_LEN, N_HEADS_A, 2 * HEAD_DIM_A), 1.0),
        'state_hgrn': nrm(ks[4], (DEC_BATCH, DEPTH, 2, N_HEADS_R, HEAD_DIM_R, HEAD_DIM_R), 0.5),
        'c': nrm(ks[5], (DEC_BATCH, D), 1.0),
        'c_ctx': nrm(ks[6], (D,), 1.0),
        'norm1_g': 1.0 + nrm(ks[7], (DEPTH, D), 0.05),
        'norm2_g': 1.0 + nrm(ks[8], (DEPTH, D), 0.05),
        'w_ada': nrm(ks[9], (DEPTH, D, N_MOD * D), 0.5 * D ** -0.5),
        'b_ada': nrm(ks[10], (DEPTH, N_MOD * D), 0.02),
        'w_in': nrm(ks[11], (DEPTH, D, PROJ_W), D ** -0.5),
        'qk_norm_g': 1.0 + nrm(ks[12], (DEPTH, 2, HEAD_DIM_A), 0.05),
        'lam': nrm(ks[13], (DEPTH, 4, HEAD_DIM_A), 0.1),
        'attn_norm_g': 1.0 + nrm(ks[14], (DEPTH, 2 * HEAD_DIM_A), 0.05),
        'w_attn_o': nrm(ks[15], (DEPTH, ATTN_W, D), ATTN_W ** -0.5),
        'conv_w': nrm(ks[16], (DEPTH, CONV_K, CONV_W), CONV_K ** -0.5),
        'conv_b': nrm(ks[17], (DEPTH, CONV_W), 0.02),
        'conv_ln_g': 1.0 + nrm(ks[18], (DEPTH, CONV_W), 0.05),
        'conv_ln_b': nrm(ks[19], (DEPTH, CONV_W), 0.02),
        'w_conv_o': nrm(ks[20], (DEPTH, CONV_W, D), CONV_W ** -0.5),
        'hgrn_lb': 1.0 + nrm(ks[21], (DEPTH, 2, REC_W), 0.1),
        'hgrn_norm_g': 1.0 + nrm(ks[22], (DEPTH, HEAD_DIM_R), 0.05),
        'w_hgrn_o': nrm(ks[23], (DEPTH, REC_W, D), REC_W ** -0.5),
        'w_out': nrm(ks[24], (DEPTH, D, D), D ** -0.5),
        'w_router': nrm(ks[25], (DEPTH, D, N_EXPERTS), D ** -0.5),
        'w_e_gate': nrm(ks[26], (DEPTH, N_EXPERTS, D, EXPERT_FF), D ** -0.5),
        'w_e_up': nrm(ks[27], (DEPTH, N_EXPERTS, D, EXPERT_FF), D ** -0.5),
        'w_e_down': nrm(ks[28], (DEPTH, N_EXPERTS, EXPERT_FF, D), EXPERT_FF ** -0.5),
    }


def reference(x_prompt, x_sample, cache_k, cache_v, state_hgrn, c, c_ctx, norm1_g, norm2_g, w_ada, b_ada,
              w_in, qk_norm_g, lam, attn_norm_g, w_attn_o, conv_w, conv_b, conv_ln_g, conv_ln_b, w_conv_o,
              hgrn_lb, hgrn_norm_g, w_hgrn_o, w_out, w_router, w_e_gate, w_e_up, w_e_down):
    sm = jax.nn.softmax(hgrn_lb.astype(jnp.float32), axis=0)
    lbs = jnp.cumsum(sm, axis=0) - sm[0]
    n_lat = x_sample.shape[1]
    rope = _axial_rope(n_lat, x_sample.dtype)
    xp, xs = x_prompt, x_sample
    new_k, new_v, new_s = [], [], []
    for l in range(DEPTH):
        lp = (norm1_g[l], norm2_g[l], w_ada[l], b_ada[l], w_in[l], qk_norm_g[l], attn_norm_g[l], w_attn_o[l],
              conv_w[l], conv_b[l], conv_ln_g[l], conv_ln_b[l], w_conv_o[l], hgrn_norm_g[l], w_hgrn_o[l],
              w_out[l], w_router[l], w_e_gate[l], w_e_up[l], w_e_down[l])
        lam_init = 0.8 - 0.6 * math.exp(-0.3 * l)
        lam_l = lam[l].astype(jnp.float32)
        lam_val = jnp.exp(jnp.sum(lam_l[0] * lam_l[1])) - jnp.exp(jnp.sum(lam_l[2] * lam_l[3])) + lam_init
        s0p = jnp.zeros((2, xp.shape[0], N_HEADS_R, HEAD_DIM_R, HEAD_DIM_R), jnp.float32)
        xp, kp, vp, sp = _layer(xp, c_ctx[None], lp, lbs[l], lam_val, lam_init, s0p)
        new_k.append(kp)
        new_v.append(vp)
        new_s.append(jnp.swapaxes(sp, 0, 1).astype(xp.dtype))
        s0s = jnp.swapaxes(state_hgrn[:, l], 0, 1).astype(jnp.float32)
        xs, _, _, _ = _layer(xs, c, lp, lbs[l], lam_val, lam_init, s0s, rope=rope,
                             ctx_kv=(cache_k[:, l], cache_v[:, l]))
    new_cache_k = jnp.stack(new_k, axis=1)
    new_cache_v = jnp.stack(new_v, axis=1)
    new_state_hgrn = jnp.stack(new_s, axis=1)
    return (xp, xs, new_cache_k, new_cache_v, new_state_hgrn)
```

```python
import functools
import math

import jax
import jax.numpy as jnp
from jax import lax
from jax.experimental import pallas as pl
from jax.experimental.pallas import tpu as pltpu

F32 = jnp.float32
BF16 = jnp.bfloat16

GRID_W = 64
N_HEADS_A = 16
HEAD_DIM_A = 64
ATTN_W = N_HEADS_A * 2 * HEAD_DIM_A
AXIS_DIM = HEAD_DIM_A // 2
ROPE_BASE = 10000.0
CONV_W = 1024
CONV_K = 31
N_HEADS_R = 8
HEAD_DIM_R = 128
REC_W = N_HEADS_R * HEAD_DIM_R
N_EXPERTS = 16
CAP_FACTOR = 2
N_BRANCH = 3
N_MOD = 6
EPS = 1e-6

LANES = 128
REC_TILE = 128
REC_SUB = 16
CONV_HALO = 16
VMEM_LIMIT = 56 * 1024 * 1024

OFF_Q = 0
OFF_K = ATTN_W
OFF_V = 2 * ATTN_W
OFF_GLU = 3 * ATTN_W
OFF_RQ = OFF_GLU + 2 * CONV_W
OFF_RI = OFF_RQ + REC_W
OFF_FF = OFF_RI + REC_W
OFF_FB = OFF_FF + REC_W
OFF_RG = OFF_FB + REC_W
OFF_GATE = OFF_RG + REC_W


def _cparams(*sem):
    return pltpu.CompilerParams(dimension_semantics=sem, vmem_limit_bytes=VMEM_LIMIT)


def _nt_dot(a, b):
    return lax.dot_general(a, b, (((1,), (1,)), ((), ())), preferred_element_type=F32)


def _dot(a, b):
    return jnp.dot(a, b, preferred_element_type=F32)


def _silu(x):
    return x * jax.nn.sigmoid(x)


class _Rows:
    def __init__(self, nb_p, seq_p, nb_s, seq_s):
        self.nb_p, self.seq_p, self.nb_s, self.seq_s = nb_p, seq_p, nb_s, seq_s
        self.P = nb_p * seq_p
        self.R = self.P + nb_s * seq_s
        assert self.P == seq_s, "prompt rows are routed as one sample-sized block"

    def mod_row(self, i, tm):
        r = i * tm
        return jnp.where(r < self.P, 0, 1 + (r - self.P) // self.seq_s)

    def seg_local(self, i, tm):
        tp, ts, npt = self.seq_p // tm, self.seq_s // tm, self.P // tm
        in_p = i < npt
        local = jnp.where(in_p, i % tp, (i - npt) % ts)
        per = jnp.where(in_p, tp, ts)
        return local, per

    def seg_index(self, i, tm):
        tp, ts, npt = self.seq_p // tm, self.seq_s // tm, self.P // tm
        return jnp.where(i < npt, i // tp, self.nb_p + (i - npt) // ts)


def _ada_kernel(c_ref, w_ref, b_ref, o_ref):
    a = _silu(c_ref[...]).astype(BF16)
    o_ref[...] = _dot(a, w_ref[...].astype(BF16)) + b_ref[...]


def _ada(cond8, w, b):
    D, N = w.shape
    tn = min(N, 512)
    return pl.pallas_call(
        _ada_kernel,
        grid=(N // tn,),
        in_specs=[pl.BlockSpec((8, D), lambda j: (0, 0)),
                  pl.BlockSpec((D, tn), lambda j: (0, j)),
                  pl.BlockSpec((1, tn), lambda j: (0, j))],
        out_specs=pl.BlockSpec((8, tn), lambda j: (0, j)),
        out_shape=jax.ShapeDtypeStruct((8, N), F32),
        compiler_params=_cparams("parallel"),
    )(cond8, w, b.reshape(1, N))


def _modulate_kernel(x_ref, g_ref, sh_ref, sc_ref, o_ref):
    x = x_ref[...]
    y = x * lax.rsqrt(jnp.mean(x * x, axis=-1, keepdims=True) + EPS) * g_ref[...]
    o_ref[...] = (y * (1.0 + sc_ref[0]) + sh_ref[0]).astype(o_ref.dtype)


def _modulate(x, g, mod, rows, i_shift, i_scale):
    R, D = x.shape
    tm = min(256, rows.seq_p)
    return pl.pallas_call(
        _modulate_kernel,
        grid=(R // tm,),
        in_specs=[pl.BlockSpec((tm, D), lambda i: (i, 0)),
                  pl.BlockSpec((1, D), lambda i: (0, 0)),
                  pl.BlockSpec((1, 1, D), lambda i: (rows.mod_row(i, tm) * N_MOD + i_shift, 0, 0)),
                  pl.BlockSpec((1, 1, D), lambda i: (rows.mod_row(i, tm) * N_MOD + i_scale, 0, 0))],
        out_specs=pl.BlockSpec((tm, D), lambda i: (i, 0)),
        out_shape=jax.ShapeDtypeStruct((R, D), BF16),
        compiler_params=_cparams("parallel"),
    )(x, g.reshape(1, D), mod, mod)


def _linear_kernel(a_ref, w_ref, o_ref):
    o_ref[...] = _dot(a_ref[...], w_ref[...].astype(BF16)).astype(o_ref.dtype)


def _linear(a, w, tm, tn, out_dtype):
    M, K = a.shape
    N = w.shape[1]
    tm, tn = min(tm, M), min(tn, N)
    return pl.pallas_call(
        _linear_kernel,
        grid=(M // tm, N // tn),
        in_specs=[pl.BlockSpec((tm, K), lambda i, j: (i, 0)),
                  pl.BlockSpec((K, tn), lambda i, j: (0, j))],
        out_specs=pl.BlockSpec((tm, tn), lambda i, j: (i, j)),
        out_shape=jax.ShapeDtypeStruct((M, N), out_dtype),
        compiler_params=_cparams("parallel", "parallel"),
    )(a, w)


def _linear_res_kernel(a_ref, w_ref, x_ref, g_ref, o_ref):
    o_ref[...] = x_ref[...] + g_ref[0] * _dot(a_ref[...], w_ref[...].astype(BF16))


def _linear_res(a, w, x, mod, rows, i_gate, tm, tn):
    M, K = a.shape
    N = w.shape[1]
    tm, tn = min(tm, rows.seq_p), min(tn, N)
    return pl.pallas_call(
        _linear_res_kernel,
        grid=(M // tm, N // tn),
        in_specs=[pl.BlockSpec((tm, K), lambda i, j: (i, 0)),
                  pl.BlockSpec((K, tn), lambda i, j: (0, j)),
                  pl.BlockSpec((tm, tn), lambda i, j: (i, j)),
                  pl.BlockSpec((1, 1, tn), lambda i, j: (rows.mod_row(i, tm) * N_MOD + i_gate, 0, j))],
        out_specs=pl.BlockSpec((tm, tn), lambda i, j: (i, j)),
        out_shape=jax.ShapeDtypeStruct((M, N), F32),
        compiler_params=_cparams("parallel", "parallel"),
    )(a, w, x, mod)


def _qkv_kernel(zq_ref, zk_ref, zv_ref, gq_ref, gk_ref, cos_ref, sin_ref, q_ref, kf_ref, kb_ref, vb_ref):
    cos, sin = cos_ref[...], sin_ref[...]
    lane = lax.broadcasted_iota(jnp.int32, (1, LANES), 1)
    upper = (lane & (AXIS_DIM // 2)) != 0
    gi = lax.broadcasted_iota(jnp.int32, (LANES, LANES), 0) // HEAD_DIM_A
    gj = lax.broadcasted_iota(jnp.int32, (LANES, LANES), 1) // HEAD_DIM_A
    group = jnp.where(gi == gj, 1.0, 0.0).astype(BF16)

    def norm_rope(x, g):
        x2 = x * x
        hi = x2.astype(BF16)
        lo = (x2 - hi.astype(F32)).astype(BF16)
        ss = _dot(hi, group) + _dot(lo, group)
        y = x * lax.rsqrt(ss * (1.0 / HEAD_DIM_A) + EPS) * g
        half = AXIS_DIM // 2
        rot = jnp.where(upper, pltpu.roll(y, half, 1), pltpu.roll(y, LANES - half, 1))
        return y * cos + rot * sin

    for h in range(N_HEADS_A):
        sl = slice(h * LANES, (h + 1) * LANES)
        q = norm_rope(zq_ref[:, sl], gq_ref[...])
        q_ref[:, sl] = (q * (HEAD_DIM_A ** -0.5)).astype(BF16)
        k = norm_rope(zk_ref[:, sl], gk_ref[...])
        kf_ref[:, sl] = k
        kb_ref[:, sl] = k.astype(BF16)
    vb_ref[...] = zv_ref[...].astype(BF16)


def _qkv_prep(z, gq, gk, cos_t, sin_t, rows):
    R = z.shape[0]
    tm = min(256, rows.seq_p)
    npt, ts = rows.P // tm, rows.seq_s // tm
    tab = lambda i: (jnp.where(i < npt, 0, 1 + (i - npt) % ts), 0)
    blk = lambda c: pl.BlockSpec((tm, ATTN_W), lambda i: (i, c))
    return pl.pallas_call(
        _qkv_kernel,
        grid=(R // tm,),
        in_specs=[blk(0), blk(1), blk(2),
                  pl.BlockSpec((1, LANES), lambda i: (0, 0)),
                  pl.BlockSpec((1, LANES), lambda i: (0, 0)),
                  pl.BlockSpec((tm, LANES), tab),
                  pl.BlockSpec((tm, LANES), tab)],
        out_specs=[blk(0)] * 4,
        out_shape=[jax.ShapeDtypeStruct((R, ATTN_W), BF16),
                   jax.ShapeDtypeStruct((R, ATTN_W), F32),
                   jax.ShapeDtypeStruct((R, ATTN_W), BF16),
                   jax.ShapeDtypeStruct((R, ATTN_W), BF16)],
        compiler_params=_cparams("parallel"),
    )(z, z, z, gq, gk, cos_t, sin_t)


def _attn_kernel(prm_ref, q_ref, k_ref, v_ref, o_ref, m_ref, l_ref, acc_ref):
    kj = pl.program_id(3)

    @pl.when(kj == 0)
    def _():
        m_ref[...] = jnp.full(m_ref.shape, -jnp.inf, F32)
        l_ref[...] = jnp.zeros(l_ref.shape, F32)
        acc_ref[...] = jnp.zeros(acc_ref.shape, F32)

    q, k, v = q_ref[0], k_ref[0], v_ref[0]
    lane = lax.broadcasted_iota(jnp.int32, (1, LANES), 1)
    for mi in range(2):
        in_map = (lane < HEAD_DIM_A) if mi == 0 else (lane >= HEAD_DIM_A)
        qm = jnp.where(in_map, q, jnp.zeros_like(q))
        s = _nt_dot(qm, k)
        m_old = m_ref[mi]
        m_new = jnp.maximum(m_old, jnp.max(s, axis=-1, keepdims=True))
        p = jnp.exp(s - m_new)
        alpha = jnp.exp(m_old - m_new)
        l_ref[mi] = alpha * l_ref[mi] + jnp.sum(p, axis=-1, keepdims=True)
        acc_ref[mi] = alpha * acc_ref[mi] + _dot(p.astype(BF16), v)
        m_ref[mi] = m_new

    @pl.when(kj == pl.num_programs(3) - 1)
    def _():
        g, lam, post = prm_ref[0:1, :], prm_ref[1:2, :], prm_ref[2:3, :]
        o = acc_ref[0] / l_ref[0] - lam * (acc_ref[1] / l_ref[1])
        y = o * lax.rsqrt(jnp.mean(o * o, axis=-1, keepdims=True) + EPS) * g
        o_ref[0] = (y * post).astype(o_ref.dtype)


def _attention(prm, q, k, v, tq, tk):
    B, Lq, _ = q.shape
    Lk = k.shape[1]
    tq, tk = min(tq, Lq), min(tk, Lk)
    return pl.pallas_call(
        _attn_kernel,
        grid=(B, N_HEADS_A, Lq // tq, Lk // tk),
        in_specs=[pl.BlockSpec((8, LANES), lambda b, h, i, j: (0, 0)),
                  pl.BlockSpec((1, tq, LANES), lambda b, h, i, j: (b, i, h)),
                  pl.BlockSpec((1, tk, LANES), lambda b, h, i, j: (b, j, h)),
                  pl.BlockSpec((1, tk, LANES), lambda b, h, i, j: (b, j, h))],
        out_specs=pl.BlockSpec((1, tq, LANES), lambda b, h, i, j: (b, i, h)),
        out_shape=jax.ShapeDtypeStruct((B, Lq, ATTN_W), BF16),
        scratch_shapes=[pltpu.VMEM((2, tq, 1), F32), pltpu.VMEM((2, tq, 1), F32),
                        pltpu.VMEM((2, tq, LANES), F32)],
        compiler_params=_cparams("parallel", "parallel", "parallel", "arbitrary"),
    )(prm, q, k, v)


def _conv_kernel(zc_ref, zp_ref, zn_ref, cw_ref, cb_ref, lng_ref, lnb_ref, o_ref, u_ref, c_ref, *, rows, tm):
    i = pl.program_id(0)
    local, per = rows.seg_local(i, tm)
    H = CONV_HALO

    def glu(z):
        return z[:, :CONV_W] * jax.nn.sigmoid(z[:, CONV_W:])

    u_ref[H:H + tm, :] = glu(zc_ref[...])
    u_ref[0:H, :] = jnp.where(local > 0, glu(zp_ref[...]), 0.0)
    u_ref[H + tm:2 * H + tm, :] = jnp.where(local < per - 1, glu(zn_ref[...]), 0.0)

    rb = min(tm, 128)

    def strip(s, carry):
        ls = pl.ds(pl.multiple_of(s * LANES, LANES), LANES)
        for r0 in range(0, tm, rb):
            acc = jnp.zeros((rb, LANES), F32)
            for j in range(CONV_K):
                start = r0 + j + H - CONV_K // 2
                acc = acc + u_ref[start:start + rb, ls] * cw_ref[j:j + 1, ls]
            c_ref[r0:r0 + rb, ls] = acc + cb_ref[:, ls]
        return carry

    lax.fori_loop(0, CONV_W // LANES, strip, 0)
    u = c_ref[...]
    mu = jnp.mean(u, axis=-1, keepdims=True)
    d = u - mu
    var = jnp.mean(d * d, axis=-1, keepdims=True)
    y = d * lax.rsqrt(var + EPS) * lng_ref[...] + lnb_ref[...]
    o_ref[...] = _silu(y).astype(o_ref.dtype)


def _conv_branch(z, cw, cb, lng, lnb, rows):
    R = z.shape[0]
    tm = min(256, rows.seq_p)
    H = CONV_HALO
    hb = tm // H
    cblk = OFF_GLU // (2 * CONV_W)
    cwp = jnp.concatenate([cw, jnp.zeros((8 - CONV_K % 8, CONV_W), F32)], axis=0)
    vec = pl.BlockSpec((1, CONV_W), lambda i: (0, 0))
    return pl.pallas_call(
        functools.partial(_conv_kernel, rows=rows, tm=tm),
        grid=(R // tm,),
        in_specs=[pl.BlockSpec((tm, 2 * CONV_W), lambda i: (i, cblk)),
                  pl.BlockSpec((H, 2 * CONV_W), lambda i: (jnp.maximum(i * hb - 1, 0), cblk)),
                  pl.BlockSpec((H, 2 * CONV_W), lambda i: (jnp.minimum((i + 1) * hb, R // H - 1), cblk)),
                  pl.BlockSpec(cwp.shape, lambda i: (0, 0)), vec, vec, vec],
        out_specs=pl.BlockSpec((tm, CONV_W), lambda i: (i, 0)),
        out_shape=jax.ShapeDtypeStruct((R, CONV_W), BF16),
        scratch_shapes=[pltpu.VMEM((tm + 2 * H, CONV_W), F32), pltpu.VMEM((tm, CONV_W), F32)],
        compiler_params=_cparams("parallel"),
    )(z, z, z, cwp, cb.reshape(1, -1), lng.reshape(1, -1), lnb.reshape(1, -1))


def _split3(x):
    a = x.astype(BF16)
    r = x - a.astype(F32)
    b = r.astype(BF16)
    c = (r - b.astype(F32)).astype(BF16)
    return a, b, c


def _hgrn_kernel(zq_ref, zi_ref, zf_ref, lb_ref, s0_ref, o_ref, sf_ref, st_ref, *, rows, reverse, nt):
    T, SUB = REC_TILE, REC_SUB
    t = pl.program_id(1)
    tt = (nt - 1 - t) if reverse else t
    local, per = rows.seg_local(tt, T)
    first = (local == per - 1) if reverse else (local == 0)

    @pl.when(first)
    def _():
        st_ref[...] = s0_ref[...].T

    zq, v, lb = zq_ref[...], zi_ref[...], lb_ref[0]
    q = _silu(zq)
    f = lb + (1.0 - lb) * jax.nn.sigmoid(zf_ref[...])
    k = 1.0 - f
    g = jnp.log(f)
    row = lax.broadcasted_iota(jnp.int32, (T, T), 0)
    col = lax.broadcasted_iota(jnp.int32, (T, T), 1)
    tri = jnp.where((col >= row) if reverse else (col <= row), 1.0, 0.0).astype(BF16)
    g1, g2, g3 = _split3(g)
    b = _dot(tri, g1) + _dot(tri, g2) + _dot(tri, g3)
    btot = jnp.sum(g, axis=0, keepdims=True)
    st = st_ref[...]
    vb = v.astype(BF16)

    o = _nt_dot((q * jnp.exp(b)).astype(BF16), st.astype(BF16))

    rowv = lax.broadcasted_iota(jnp.int32, (T, 1), 0)
    sub_row = lax.broadcasted_iota(jnp.int32, (SUB, 1), 0)
    lane = lax.broadcasted_iota(jnp.int32, (1, T), 1)
    blocks = []
    for i in range(T // SUB):
        lo, hi = SUB * i, SUB * (i + 1)
        bi, qi = b[lo:hi], q[lo:hi]
        has_off = (i < T // SUB - 1) if reverse else (i > 0)
        if has_off:
            r = b[hi:hi + 1] if reverse else b[lo - 1:lo]
            qt = (qi * jnp.exp(bi - r)).astype(BF16)
            kt = k * jnp.exp(jnp.minimum(r - b, 0.0))
            kt = jnp.where((rowv >= hi) if reverse else (rowv < lo), kt, 0.0).astype(BF16)
            a_i = _nt_dot(qt, kt)
        else:
            a_i = jnp.zeros((SUB, T), F32)
        for s in range(SUB):
            rs = lo + s
            e = jnp.exp(jnp.minimum(bi - b[rs:rs + 1], 0.0))
            cv = jnp.sum(qi * e * k[rs:rs + 1], axis=-1, keepdims=True)
            cv = jnp.where((sub_row <= s) if reverse else (sub_row >= s), cv, 0.0)
            a_i = jnp.where(lane == rs, cv, a_i)
        blocks.append(a_i)
    a = jnp.concatenate(blocks, axis=0)
    o_ref[...] = o + _dot(a.astype(BF16), vb)

    kd = (k * jnp.exp(btot - b)).astype(BF16)
    st_new = st * jnp.exp(btot) + _dot(v.T.astype(BF16), kd)
    st_ref[...] = st_new
    sf_ref[...] = st_new.T


def _hgrn_scan(z, lb16, s0, rows, reverse):
    R = z.shape[0]
    T = REC_TILE
    nt = R // T
    d = 1 if reverse else 0
    nseg = rows.nb_p + rows.nb_s
    tile = (lambda t: nt - 1 - t) if reverse else (lambda t: t)
    zblk = lambda off: pl.BlockSpec((T, LANES), lambda h, t: (tile(t), off // LANES + h))
    return pl.pallas_call(
        functools.partial(_hgrn_kernel, rows=rows, reverse=reverse, nt=nt),
        grid=(N_HEADS_R, nt),
        in_specs=[zblk(OFF_RQ), zblk(OFF_RI), zblk(OFF_FB if reverse else OFF_FF),
                  pl.BlockSpec((1, 1, LANES), lambda h, t: (d * N_HEADS_R + h, 0, 0)),
                  pl.BlockSpec((None, None, None, HEAD_DIM_R, HEAD_DIM_R),
                               lambda h, t: (d, rows.seg_index(tile(t), T), h, 0, 0))],
        out_specs=[pl.BlockSpec((T, LANES), lambda h, t: (tile(t), h)),
                   pl.BlockSpec((None, None, HEAD_DIM_R, HEAD_DIM_R),
                                lambda h, t: (rows.seg_index(tile(t), T), h, 0, 0))],
        out_shape=[jax.ShapeDtypeStruct((R, REC_W), F32),
                   jax.ShapeDtypeStruct((nseg, N_HEADS_R, HEAD_DIM_R, HEAD_DIM_R), F32)],
        scratch_shapes=[pltpu.VMEM((HEAD_DIM_R, HEAD_DIM_R), F32)],
        compiler_params=_cparams("parallel", "arbitrary"),
    )(z, z, z, lb16, s0)


def _rec_post_kernel(of_ref, ob_ref, zg_ref, g_ref, o_ref):
    for h in range(N_HEADS_R):
        sl = slice(h * LANES, (h + 1) * LANES)
        o = of_ref[:, sl] + ob_ref[:, sl]
        y = o * lax.rsqrt(jnp.mean(o * o, axis=-1, keepdims=True) + EPS) * g_ref[...]
        o_ref[:, sl] = (y * _silu(zg_ref[:, sl])).astype(o_ref.dtype)


def _rec_post(o_f, o_b, z, g, rows):
    R = z.shape[0]
    tm = min(256, rows.seq_p)
    blk = pl.BlockSpec((tm, REC_W), lambda i: (i, 0))
    return pl.pallas_call(
        _rec_post_kernel,
        grid=(R // tm,),
        in_specs=[blk, blk, pl.BlockSpec((tm, REC_W), lambda i: (i, OFF_RG // REC_W)),
                  pl.BlockSpec((1, LANES), lambda i: (0, 0))],
        out_specs=blk,
        out_shape=jax.ShapeDtypeStruct((R, REC_W), BF16),
        compiler_params=_cparams("parallel"),
    )(o_f, o_b, z, g.reshape(1, LANES))


def _merge_kernel(ua_ref, uc_ref, uh_ref, wa_ref, wc_ref, wh_ref, ga_ref, gc_ref, gh_ref, o_ref):
    m = jax.nn.sigmoid(ga_ref[...]) * _dot(ua_ref[...], wa_ref[...].astype(BF16))
    m = m + jax.nn.sigmoid(gc_ref[...]) * _dot(uc_ref[...], wc_ref[...].astype(BF16))
    m = m + jax.nn.sigmoid(gh_ref[...]) * _dot(uh_ref[...], wh_ref[...].astype(BF16))
    o_ref[...] = m.astype(o_ref.dtype)


def _merge(ua, uc, uh, wa, wc, wh, z):
    R = ua.shape[0]
    D = wa.shape[1]
    tm, tn = min(512, R), min(512, D)
    gb = OFF_GATE // tn
    nd = D // tn
    act = lambda w: pl.BlockSpec((tm, w), lambda j, i: (i, 0))
    wgt = lambda w: pl.BlockSpec((w, tn), lambda j, i: (0, j))
    gate = lambda br: pl.BlockSpec((tm, tn), lambda j, i: (i, gb + br * nd + j))
    return pl.pallas_call(
        _merge_kernel,
        grid=(D // tn, R // tm),
        in_specs=[act(ATTN_W), act(CONV_W), act(REC_W), wgt(ATTN_W), wgt(CONV_W), wgt(REC_W),
                  gate(0), gate(1), gate(2)],
        out_specs=pl.BlockSpec((tm, tn), lambda j, i: (i, j)),
        out_shape=jax.ShapeDtypeStruct((R, D), BF16),
        compiler_params=_cparams("parallel", "parallel"),
    )(ua, uc, uh, wa, wc, wh, z, z, z)


def _router_kernel(w_ref, h_ref, o_ref):
    s = _nt_dot(w_ref[...], h_ref[...])
    e = jnp.exp(s - jnp.max(s, axis=0, keepdims=True))
    o_ref[...] = e / jnp.sum(e, axis=0, keepdims=True)


def _router(h, w_router):
    R, D = h.shape
    tm = min(512, R)
    wt = w_router.T.astype(BF16)
    return pl.pallas_call(
        _router_kernel,
        grid=(R // tm,),
        in_specs=[pl.BlockSpec((N_EXPERTS, D), lambda i: (0, 0)),
                  pl.BlockSpec((tm, D), lambda i: (i, 0))],
        out_specs=pl.BlockSpec((N_EXPERTS, tm), lambda i: (0, i)),
        out_shape=jax.ShapeDtypeStruct((N_EXPERTS, R), F32),
        compiler_params=_cparams("parallel"),
    )(wt, h)


def _topk_kernel(a_ref, slot_ref, gate_ref, *, n, cap, blocks_per_route):
    x = a_ref[...]
    bits = pltpu.bitcast(x, jnp.int32)

    def search(i, thr):
        cand = thr | (1 << (30 - i))
        cnt = jnp.sum(jnp.where(bits >= cand, 1.0, 0.0), axis=-1, keepdims=True)
        return jnp.where(cnt >= cap, cand, thr)

    thr = lax.fori_loop(0, 31, search, jnp.zeros((N_EXPERTS, 1), jnp.int32))
    gt = bits > thr
    eq = bits == thr
    need = cap - jnp.sum(jnp.where(gt, 1.0, 0.0), axis=-1, keepdims=True)

    cb = min(n, 512)
    ci = lax.broadcasted_iota(jnp.int32, (cb, cb), 0)
    cj = lax.broadcasted_iota(jnp.int32, (cb, cb), 1)
    upper = jnp.where(ci <= cj, 1.0, 0.0).astype(BF16)

    def prefix(mask01):
        outs, run = [], jnp.zeros((N_EXPERTS, 1), F32)
        for c in range(n // cb):
            inc = _dot(mask01[:, c * cb:(c + 1) * cb].astype(BF16), upper) + run
            outs.append(inc)
            run = inc[:, cb - 1:cb]
        return jnp.concatenate(outs, axis=1) if len(outs) > 1 else outs[0]

    eq01 = jnp.where(eq, 1.0, 0.0)
    sel = gt | (eq & (prefix(eq01) - eq01 < need))
    sel01 = jnp.where(sel, 1.0, 0.0)
    pos = prefix(sel01) - sel01
    base = (pl.program_id(0) % blocks_per_route) * cap
    slot_ref[...] = jnp.where(sel, pos.astype(jnp.int32) + base, -1)
    gate_ref[...] = jnp.where(sel, x, 0.0)


def _topk(aff_t, n, cap, col0, ncols, blocks_per_route):
    nb = ncols // n
    off = col0 // n
    blk_in = pl.BlockSpec((N_EXPERTS, n), lambda s: (0, off + s))
    blk = pl.BlockSpec((N_EXPERTS, n), lambda s: (0, s))
    return pl.pallas_call(
        functools.partial(_topk_kernel, n=n, cap=cap, blocks_per_route=blocks_per_route),
        grid=(nb,),
        in_specs=[blk_in],
        out_specs=[blk, blk],
        out_shape=[jax.ShapeDtypeStruct((N_EXPERTS, ncols), jnp.int32),
                   jax.ShapeDtypeStruct((N_EXPERTS, ncols), F32)],
        compiler_params=_cparams("parallel"),
    )(aff_t)


def _gather_kernel(slot_ref, h_ref, x_ref, acc_ref, *, cap):
    e, kt = pl.program_id(1), pl.program_id(2)

    @pl.when(kt == 0)
    def _():
        acc_ref[...] = jnp.zeros(acc_ref.shape, F32)

    slots = slot_ref[pl.ds(e, 1), :]
    p = lax.broadcasted_iota(jnp.int32, (cap, 1), 0)
    onehot = jnp.where(slots == p, 1.0, 0.0).astype(BF16)
    acc_ref[...] += _dot(onehot, h_ref[...])

    @pl.when(kt == pl.num_programs(2) - 1)
    def _():
        x_ref[0] = acc_ref[...].astype(x_ref.dtype)


def _gather(slot, h, n, cap):
    R, D = h.shape
    nseg = R // n
    tk = min(512, n)
    return pl.pallas_call(
        functools.partial(_gather_kernel, cap=cap),
        grid=(nseg, N_EXPERTS, n // tk),
        in_specs=[pl.BlockSpec((N_EXPERTS, tk), lambda s, e, k: (0, s * (n // tk) + k)),
                  pl.BlockSpec((tk, D), lambda s, e, k: (s * (n // tk) + k, 0))],
        out_specs=pl.BlockSpec((1, cap, D), lambda s, e, k: (e, s, 0)),
        out_shape=jax.ShapeDtypeStruct((N_EXPERTS, nseg * cap, D), BF16),
        scratch_shapes=[pltpu.VMEM((cap, D), F32)],
        compiler_params=_cparams("parallel", "parallel", "arbitrary"),
    )(slot, h)


def _ffn_up_kernel(x_ref, wg_ref, wu_ref, o_ref):
    x = x_ref[0]
    a = _dot(x, wg_ref[0].astype(BF16))
    b = _dot(x, wu_ref[0].astype(BF16))
    o_ref[0] = (_silu(a) * b).astype(o_ref.dtype)


def _ffn_up(x, wg, wu):
    E, M, D = x.shape
    Fh = wg.shape[2]
    tm = M // 2 if (M // 2) % 16 == 0 else M
    tf = min(256, Fh)
    wblk = pl.BlockSpec((1, D, tf), lambda e, i, j: (e, 0, j))
    return pl.pallas_call(
        _ffn_up_kernel,
        grid=(E, M // tm, Fh // tf),
        in_specs=[pl.BlockSpec((1, tm, D), lambda e, i, j: (e, i, 0)), wblk, wblk],
        out_specs=pl.BlockSpec((1, tm, tf), lambda e, i, j: (e, i, j)),
        out_shape=jax.ShapeDtypeStruct((E, M, Fh), BF16),
        compiler_params=_cparams("parallel", "parallel", "parallel"),
    )(x, wg, wu)


def _ffn_down_kernel(h_ref, w_ref, o_ref):
    o_ref[0] = _dot(h_ref[0], w_ref[0].astype(BF16)).astype(o_ref.dtype)


def _ffn_down(hid, wd):
    E, M, Fh = hid.shape
    D = wd.shape[2]
    tm = M // 2 if (M // 2) % 16 == 0 else M
    tn = min(1024, D)
    return pl.pallas_call(
        _ffn_down_kernel,
        grid=(E, M // tm, D // tn),
        in_specs=[pl.BlockSpec((1, tm, Fh), lambda e, i, j: (e, i, 0)),
                  pl.BlockSpec((1, Fh, tn), lambda e, i, j: (e, 0, j))],
        out_specs=pl.BlockSpec((1, tm, tn), lambda e, i, j: (e, i, j)),
        out_shape=jax.ShapeDtypeStruct((E, M, D), BF16),
        compiler_params=_cparams("parallel", "parallel", "parallel"),
    )(hid, wd)


def _combine_kernel(slot_ref, gate_ref, y_ref, x_ref, g_ref, o_ref, acc_ref, *, cap):
    e = pl.program_id(2)

    @pl.when(e == 0)
    def _():
        acc_ref[...] = jnp.zeros(acc_ref.shape, F32)

    lane_e = lax.broadcasted_iota(jnp.int32, (1, N_EXPERTS), 1) == e
    slot = jnp.sum(jnp.where(lane_e, slot_ref[...], 0.0), axis=-1, keepdims=True)
    gate = jnp.sum(jnp.where(lane_e, gate_ref[...], 0.0), axis=-1, keepdims=True)
    p = lax.broadcasted_iota(jnp.int32, (1, cap), 1).astype(F32)
    onehot = jnp.where(slot == p, 1.0, 0.0).astype(BF16)
    acc_ref[...] += gate * _dot(onehot, y_ref[0])

    @pl.when(e == pl.num_programs(2) - 1)
    def _():
        o_ref[...] = x_ref[...] + g_ref[0] * acc_ref[...]


def _combine(slot_t, gate_t, y, x, mod, rows, i_gate, n, cap):
    R, D = x.shape
    T = min(256, n)
    per = n // T
    tok = lambda w: pl.BlockSpec((T, w), lambda s, i, e: (s * per + i, 0))
    return pl.pallas_call(
        functools.partial(_combine_kernel, cap=cap),
        grid=(R // n, per, N_EXPERTS),
        in_specs=[tok(N_EXPERTS), tok(N_EXPERTS),
                  pl.BlockSpec((1, cap, D), lambda s, i, e: (e, s, 0)),
                  tok(D),
                  pl.BlockSpec((1, 1, D), lambda s, i, e: (rows.mod_row(s, n) * N_MOD + i_gate, 0, 0))],
        out_specs=tok(D),
        out_shape=jax.ShapeDtypeStruct((R, D), F32),
        scratch_shapes=[pltpu.VMEM((T, D), F32)],
        compiler_params=_cparams("parallel", "parallel", "arbitrary"),
    )(slot_t, gate_t, y, x, mod)


def _moe(h2, x, mod, rows, w_router, w_gate, w_up, w_down):
    P, S = rows.P, rows.seq_s
    aff_t = _router(h2, w_router)
    cap_p = CAP_FACTOR * rows.seq_p // N_EXPERTS
    cap_s = CAP_FACTOR * S // N_EXPERTS
    slot_p, gate_p = _topk(aff_t, rows.seq_p, cap_p, 0, P, rows.nb_p)
    slot_s, gate_s = _topk(aff_t, S, cap_s, P, rows.R - P, 1)
    slot = jnp.concatenate([slot_p, slot_s], axis=1)
    gate = jnp.concatenate([gate_p, gate_s], axis=1)
    xs = _gather(slot, h2, S, cap_s)
    ys = _ffn_down(_ffn_up(xs, w_gate, w_up), w_down)
    return _combine(slot.T.astype(F32), gate.T, ys, x, mod, rows, 5, S, cap_s)


def _layer(x, l, rows, mod, p, tables, lam_val, lam_init, lbs, cache_k, cache_v, state_hgrn):
    R, D = x.shape
    P = rows.P
    h = _modulate(x, p["norm1_g"][l], mod, rows, 0, 1)
    z = _linear(h, p["w_in"][l], 1024, 512, F32)

    tile2 = lambda g: jnp.tile(g, 2).reshape(1, LANES)
    q_b, k_f, k_b, v_b = _qkv_prep(z, tile2(p["qk_norm_g"][l, 0]), tile2(p["qk_norm_g"][l, 1]),
                                   tables[0], tables[1], rows)
    prm = jnp.zeros((8, LANES), F32)
    prm = prm.at[0].set(p["attn_norm_g"][l]).at[1].set(lam_val).at[2].set(1.0 - lam_init)
    seg = lambda a, nb: a.reshape(nb, -1, ATTN_W)
    o_p = _attention(prm, seg(q_b[:P], rows.nb_p), seg(k_b[:P], rows.nb_p), seg(v_b[:P], rows.nb_p), 256, 256)
    kk = jnp.concatenate([cache_k[:, l].reshape(rows.nb_s, -1, ATTN_W).astype(BF16), seg(k_b[P:], rows.nb_s)], axis=1)
    vv = jnp.concatenate([cache_v[:, l].reshape(rows.nb_s, -1, ATTN_W).astype(BF16), seg(v_b[P:], rows.nb_s)], axis=1)
    o_s = _attention(prm, seg(q_b[P:], rows.nb_s), kk, vv, 512, 512)
    ua = jnp.concatenate([o_p.reshape(P, ATTN_W), o_s.reshape(R - P, ATTN_W)], axis=0)

    uc = _conv_branch(z, p["conv_w"][l], p["conv_b"][l], p["conv_ln_g"][l], p["conv_ln_b"][l], rows)

    s0 = jnp.concatenate([jnp.zeros((2, rows.nb_p, N_HEADS_R, HEAD_DIM_R, HEAD_DIM_R), F32),
                          jnp.swapaxes(state_hgrn[:, l], 0, 1).astype(F32)], axis=1)
    lb16 = lbs[l].reshape(2 * N_HEADS_R, 1, HEAD_DIM_R)
    o_f, s_f = _hgrn_scan(z, lb16, s0, rows, False)
    o_b, s_b = _hgrn_scan(z, lb16, s0, rows, True)
    uh = _rec_post(o_f, o_b, z, p["hgrn_norm_g"][l], rows)

    m = _merge(ua, uc, uh, p["w_attn_o"][l], p["w_conv_o"][l], p["w_hgrn_o"][l], z)
    x = _linear_res(m, p["w_out"][l], x, mod, rows, 2, 1024, 512)

    h2 = _modulate(x, p["norm2_g"][l], mod, rows, 3, 4)
    x = _moe(h2, x, mod, rows, p["w_router"][l], p["w_e_gate"][l], p["w_e_up"][l], p["w_e_down"][l])

    new_k = k_f[:P].reshape(rows.nb_p, rows.seq_p, N_HEADS_A, 2, HEAD_DIM_A)
    new_v = z[:P, OFF_V:OFF_V + ATTN_W].reshape(rows.nb_p, rows.seq_p, N_HEADS_A, 2 * HEAD_DIM_A)
    new_s = jnp.stack([s_f[:rows.nb_p], s_b[:rows.nb_p]], axis=1)
    return x, new_k, new_v, new_s


def _rope_tables(n_tok, tm):
    t = jnp.arange(n_tok)
    row = (t // GRID_W).astype(F32)
    col = (t % GRID_W).astype(F32)
    inv = ROPE_BASE ** (-jnp.arange(0, AXIS_DIM, 2, dtype=F32) / AXIS_DIM)
    ar, ac = row[:, None] * inv, col[:, None] * inv
    ang = jnp.concatenate([ar, ar, ac, ac], axis=-1)
    sign = jnp.where((jnp.arange(HEAD_DIM_A) & (AXIS_DIM // 2)) != 0, 1.0, -1.0).astype(F32)
    cos = jnp.tile(jnp.cos(ang), (1, 2))
    sin = jnp.tile(jnp.sin(ang) * sign, (1, 2))
    cos = jnp.concatenate([jnp.ones((tm, LANES), F32), cos], axis=0)
    sin = jnp.concatenate([jnp.zeros((tm, LANES), F32), sin], axis=0)
    return cos, sin


def kernel(x_prompt, x_sample, cache_k, cache_v, state_hgrn, c, c_ctx, norm1_g, norm2_g, w_ada, b_ada, w_in, qk_norm_g, lam, attn_norm_g, w_attn_o, conv_w, conv_b, conv_ln_g, conv_ln_b, w_conv_o, hgrn_lb, hgrn_norm_g, w_hgrn_o, w_out, w_router, w_e_gate, w_e_up, w_e_down):
    nb_p, seq_p, D = x_prompt.shape
    nb_s, seq_s, _ = x_sample.shape
    depth = w_in.shape[0]
    rows = _Rows(nb_p, seq_p, nb_s, seq_s)
    p = dict(norm1_g=norm1_g, norm2_g=norm2_g, w_in=w_in, qk_norm_g=qk_norm_g, attn_norm_g=attn_norm_g,
             w_attn_o=w_attn_o, conv_w=conv_w, conv_b=conv_b, conv_ln_g=conv_ln_g, conv_ln_b=conv_ln_b,
             w_conv_o=w_conv_o, hgrn_norm_g=hgrn_norm_g, w_hgrn_o=w_hgrn_o, w_out=w_out, w_router=w_router,
             w_e_gate=w_e_gate, w_e_up=w_e_up, w_e_down=w_e_down)

    sm = jax.nn.softmax(hgrn_lb.astype(F32), axis=0)
    lbs = jnp.cumsum(sm, axis=0) - sm[0]
    tables = _rope_tables(seq_s, min(256, seq_p))
    cond8 = jnp.concatenate([c_ctx[None], c, jnp.zeros((8 - 1 - nb_s, D), F32)], axis=0)

    x = jnp.concatenate([x_prompt.reshape(rows.P, D), x_sample.reshape(nb_s * seq_s, D)], axis=0)
    new_k, new_v, new_s = [], [], []
    for l in range(depth):
        lam_init = 0.8 - 0.6 * math.exp(-0.3 * l)
        lam_l = lam[l].astype(F32)
        lam_val = jnp.exp(jnp.sum(lam_l[0] * lam_l[1])) - jnp.exp(jnp.sum(lam_l[2] * lam_l[3])) + lam_init
        mod = _ada(cond8, w_ada[l], b_ada[l]).reshape(8 * N_MOD, 1, D)
        x, k_l, v_l, s_l = _layer(x, l, rows, mod, p, tables, lam_val, lam_init, lbs, cache_k, cache_v, state_hgrn)
        new_k.append(k_l)
        new_v.append(v_l)
        new_s.append(s_l)
    y_prompt = x[:rows.P].reshape(nb_p, seq_p, D)
    y_sample = x[rows.P:].reshape(nb_s, seq_s, D)
    return (y_prompt, y_sample, jnp.stack(new_k, axis=1), jnp.stack(new_v, axis=1), jnp.stack(new_s, axis=1))
```

```python
import functools
import math

import jax
import jax.numpy as jnp
from jax import lax
from jax.experimental import pallas as pl
from jax.experimental.pallas import tpu as pltpu

F32 = jnp.float32
BF16 = jnp.bfloat16

GRID_W = 64
N_HEADS_A = 16
HEAD_DIM_A = 64
ATTN_W = N_HEADS_A * 2 * HEAD_DIM_A
AXIS_DIM = HEAD_DIM_A // 2
ROPE_BASE = 10000.0
CONV_W = 1024
CONV_K = 31
N_HEADS_R = 8
HEAD_DIM_R = 128
REC_W = N_HEADS_R * HEAD_DIM_R
N_EXPERTS = 16
CAP_FACTOR = 2
N_BRANCH = 3
N_MOD = 6
EPS = 1e-6

LANES = 128
REC_TILE = 128
REC_SUB = 16
CONV_HALO = 16
VMEM_LIMIT = 56 * 1024 * 1024

OFF_Q = 0
OFF_K = ATTN_W
OFF_V = 2 * ATTN_W
OFF_GLU = 3 * ATTN_W
OFF_RQ = OFF_GLU + 2 * CONV_W
OFF_RI = OFF_RQ + REC_W
OFF_FF = OFF_RI + REC_W
OFF_FB = OFF_FF + REC_W
OFF_RG = OFF_FB + REC_W
OFF_GATE = OFF_RG + REC_W


def _cparams(*sem):
    return pltpu.CompilerParams(dimension_semantics=sem, vmem_limit_bytes=VMEM_LIMIT)


def _nt_dot(a, b):
    return lax.dot_general(a, b, (((1,), (1,)), ((), ())), preferred_element_type=F32)


def _dot(a, b):
    return jnp.dot(a, b, preferred_element_type=F32)


def _silu(x):
    return x * jax.nn.sigmoid(x)


class _Rows:
    def __init__(self, nb_p, seq_p, nb_s, seq_s):
        self.nb_p, self.seq_p, self.nb_s, self.seq_s = nb_p, seq_p, nb_s, seq_s
        self.P = nb_p * seq_p
        self.R = self.P + nb_s * seq_s
        assert self.P == seq_s, "prompt rows are routed as one sample-sized block"

    def mod_row(self, i, tm):
        r = i * tm
        return jnp.where(r < self.P, 0, 1 + (r - self.P) // self.seq_s)

    def seg_local(self, i, tm):
        tp, ts, npt = self.seq_p // tm, self.seq_s // tm, self.P // tm
        in_p = i < npt
        local = jnp.where(in_p, i % tp, (i - npt) % ts)
        per = jnp.where(in_p, tp, ts)
        return local, per

    def seg_index(self, i, tm):
        tp, ts, npt = self.seq_p // tm, self.seq_s // tm, self.P // tm
        return jnp.where(i < npt, i // tp, self.nb_p + (i - npt) // ts)


def _ada_kernel(c_ref, w_ref, b_ref, o_ref):
    a = _silu(c_ref[...]).astype(BF16)
    o_ref[...] = _dot(a, w_ref[...].astype(BF16)) + b_ref[...]


def _ada(cond8, w, b, l):
    _, D, N = w.shape
    tn = min(N, 512)
    return pl.pallas_call(
        _ada_kernel,
        grid=(N // tn,),
        in_specs=[pl.BlockSpec((8, D), lambda j: (0, 0)),
                  pl.BlockSpec((None, D, tn), lambda j: (l, 0, j)),
                  pl.BlockSpec((None, 1, tn), lambda j: (l, 0, j))],
        out_specs=pl.BlockSpec((8, tn), lambda j: (0, j)),
        out_shape=jax.ShapeDtypeStruct((8, N), F32),
        compiler_params=_cparams("parallel"),
    )(cond8, w, b.reshape(b.shape[0], 1, N))


def _modulate_kernel(x_ref, g_ref, sh_ref, sc_ref, o_ref):
    x = x_ref[...]
    y = x * lax.rsqrt(jnp.mean(x * x, axis=-1, keepdims=True) + EPS) * g_ref[...]
    o_ref[...] = (y * (1.0 + sc_ref[0]) + sh_ref[0]).astype(o_ref.dtype)


def _modulate(x, g, mod, rows, i_shift, i_scale):
    R, D = x.shape
    tm = min(256, rows.seq_p)
    return pl.pallas_call(
        _modulate_kernel,
        grid=(R // tm,),
        in_specs=[pl.BlockSpec((tm, D), lambda i: (i, 0)),
                  pl.BlockSpec((1, D), lambda i: (0, 0)),
                  pl.BlockSpec((1, 1, D), lambda i: (rows.mod_row(i, tm) * N_MOD + i_shift, 0, 0)),
                  pl.BlockSpec((1, 1, D), lambda i: (rows.mod_row(i, tm) * N_MOD + i_scale, 0, 0))],
        out_specs=pl.BlockSpec((tm, D), lambda i: (i, 0)),
        out_shape=jax.ShapeDtypeStruct((R, D), BF16),
        compiler_params=_cparams("parallel"),
    )(x, g.reshape(1, D), mod, mod)


def _linear_kernel(a_ref, w_ref, o_ref):
    o_ref[...] = _dot(a_ref[...], w_ref[...].astype(BF16)).astype(o_ref.dtype)


def _linear(a, w, l, tm, tn, out_dtype):
    M, K = a.shape
    N = w.shape[2]
    tm, tn = min(tm, M), min(tn, N)
    return pl.pallas_call(
        _linear_kernel,
        grid=(M // tm, N // tn),
        in_specs=[pl.BlockSpec((tm, K), lambda i, j: (i, 0)),
                  pl.BlockSpec((None, K, tn), lambda i, j: (l, 0, j))],
        out_specs=pl.BlockSpec((tm, tn), lambda i, j: (i, j)),
        out_shape=jax.ShapeDtypeStruct((M, N), out_dtype),
        compiler_params=_cparams("parallel", "parallel"),
    )(a, w)


def _linear_res_kernel(a_ref, w_ref, x_ref, g_ref, o_ref):
    o_ref[...] = x_ref[...] + g_ref[0] * _dot(a_ref[...], w_ref[...])


def _linear_res(a, w, x, mod, rows, i_gate, tm, tn):
    M, K = a.shape
    N = w.shape[1]
    tm, tn = min(tm, rows.P), min(tn, N)
    return pl.pallas_call(
        _linear_res_kernel,
        grid=(N // tn, M // tm),
        in_specs=[pl.BlockSpec((tm, K), lambda j, i: (i, 0)),
                  pl.BlockSpec((K, tn), lambda j, i: (0, j)),
                  pl.BlockSpec((tm, tn), lambda j, i: (i, j)),
                  pl.BlockSpec((1, 1, tn), lambda j, i: (rows.mod_row(i, tm) * N_MOD + i_gate, 0, j))],
        out_specs=pl.BlockSpec((tm, tn), lambda j, i: (i, j)),
        out_shape=jax.ShapeDtypeStruct((M, N), F32),
        compiler_params=_cparams("parallel", "parallel"),
    )(a, w, x, mod)


LOG2E = math.log2(math.e)
Q_SCALE = HEAD_DIM_A ** -0.5 * LOG2E
FAST_SOFTMAX_MAX_OFFSET = 55.0


def _qkv_kernel(zq_ref, zk_ref, zv_ref, gq_ref, gk_ref, pad_ref, cos_ref, sin_ref,
                qa_ref, qb_ref, ka_ref, kb_ref, kf_ref, vx_ref):
    cos, sin = cos_ref[...], sin_ref[...]
    lane = lax.broadcasted_iota(jnp.int32, (1, LANES), 1)
    upper = (lane & (AXIS_DIM // 2)) != 0
    lower_map = lane < HEAD_DIM_A
    gi = lax.broadcasted_iota(jnp.int32, (LANES, LANES), 0) // HEAD_DIM_A
    gj = lax.broadcasted_iota(jnp.int32, (LANES, LANES), 1) // HEAD_DIM_A
    group = jnp.where(gi == gj, 1.0, 0.0).astype(BF16)
    qpad, kpad = pad_ref[0:1, :], pad_ref[1:2, :]
    ones = jnp.ones((zq_ref.shape[0], LANES), BF16)

    def norm_rope(x, g):
        x2 = x * x
        hi = x2.astype(BF16)
        lo = (x2 - hi.astype(F32)).astype(BF16)
        ss = _dot(hi, group) + _dot(lo, group)
        y = x * lax.rsqrt(ss * (1.0 / HEAD_DIM_A) + EPS) * g
        half = AXIS_DIM // 2
        rot = jnp.where(upper, pltpu.roll(y, half, 1), pltpu.roll(y, LANES - half, 1))
        return y * cos + rot * sin

    def split_maps(y, pad, a_ref, b_ref, sl):
        a_ref[:, sl] = jnp.where(lower_map, y, pad).astype(BF16)
        b_ref[:, sl] = jnp.where(lower_map, pltpu.roll(y, HEAD_DIM_A, 1), pad).astype(BF16)

    for h in range(N_HEADS_A):
        sl = slice(h * LANES, (h + 1) * LANES)
        q = norm_rope(zq_ref[:, sl], gq_ref[...]) * Q_SCALE
        split_maps(q, qpad, qa_ref, qb_ref, sl)
        k = norm_rope(zk_ref[:, sl], gk_ref[...])
        kf_ref[:, sl] = k
        split_maps(k, kpad, ka_ref, kb_ref, sl)
        vx_ref[:, 2 * h * LANES:(2 * h + 1) * LANES] = zv_ref[:, sl].astype(BF16)
        vx_ref[:, (2 * h + 1) * LANES:(2 * h + 2) * LANES] = ones


def _qkv_prep(z, gq, gk, pad, cos_t, sin_t, rows):
    R = z.shape[0]
    tm = min(256, rows.seq_p)
    npt, ts = rows.P // tm, rows.seq_s // tm
    tab = lambda i: (jnp.where(i < npt, 0, 1 + (i - npt) % ts), 0)
    blk = lambda c: pl.BlockSpec((tm, ATTN_W), lambda i: (i, c))
    bf = jax.ShapeDtypeStruct((R, ATTN_W), BF16)
    return pl.pallas_call(
        _qkv_kernel,
        grid=(R // tm,),
        in_specs=[blk(0), blk(1), blk(2),
                  pl.BlockSpec((1, LANES), lambda i: (0, 0)),
                  pl.BlockSpec((1, LANES), lambda i: (0, 0)),
                  pl.BlockSpec((8, LANES), lambda i: (0, 0)),
                  pl.BlockSpec((tm, LANES), tab),
                  pl.BlockSpec((tm, LANES), tab)],
        out_specs=[blk(0)] * 5 + [pl.BlockSpec((tm, 2 * ATTN_W), lambda i: (i, 0))],
        out_shape=[bf, bf, bf, bf, jax.ShapeDtypeStruct((R, ATTN_W), F32),
                   jax.ShapeDtypeStruct((R, 2 * ATTN_W), BF16)],
        compiler_params=_cparams("parallel"),
    )(z, z, z, gq, gk, pad, cos_t, sin_t)


def _attn_kernel(prm_ref, qa_ref, qb_ref, ka_ref, kb_ref, vx_ref, *rest, tk, n_ctx, n_self, online):
    if n_ctx:
        kac_ref, kbc_ref, vxc_ref, o_ref, acc_ref, m_ref = rest
    else:
        o_ref, acc_ref, m_ref = rest
    acc_ref[...] = jnp.zeros(acc_ref.shape, F32)
    if online:
        m_ref[...] = jnp.full(m_ref.shape, -jnp.inf, F32)
    qs = (qa_ref[...], qb_ref[...])

    def chunk(k_refs, v_ref, j):
        ks = pl.ds(pl.multiple_of(j * tk, tk), tk)
        v = v_ref[ks, :]
        for mi in range(2):
            s = _nt_dot(qs[mi], k_refs[mi][ks, :])
            if online:
                m_old = m_ref[mi]
                m_new = jnp.maximum(m_old, jnp.max(s, axis=-1, keepdims=True))
                p = jnp.exp2(s - m_new).astype(BF16)
                acc_ref[mi] = jnp.exp2(m_old - m_new) * acc_ref[mi] + _dot(p, v)
                m_ref[mi] = m_new
            else:
                acc_ref[mi] += _dot(jnp.exp2(s).astype(BF16), v)

    def loop(n, k_refs, v_ref):
        def body(j, carry):
            chunk(k_refs, v_ref, j)
            return carry
        lax.fori_loop(0, n, body, 0)

    if n_ctx:
        loop(n_ctx, (kac_ref, kbc_ref), vxc_ref)
    loop(n_self, (ka_ref, kb_ref), vx_ref)

    g, lam, post = prm_ref[0:1, :], prm_ref[1:2, :], prm_ref[2:3, :]
    a0, a1 = acc_ref[0], acc_ref[1]
    o = a0[:, :LANES] / a0[:, LANES:] - lam * (a1[:, :LANES] / a1[:, LANES:])
    y = o * lax.rsqrt(jnp.mean(o * o, axis=-1, keepdims=True) + EPS) * g
    o_ref[...] = (y * post).astype(o_ref.dtype)


def _attention(prm, qa, qb, ka, kb, vx, ctx, row0, nb, L, tq, tk, fast_ok):
    tq, tk = min(tq, L), min(tk, L)
    qt = L // tq
    qspec = pl.BlockSpec((tq, LANES), lambda b, h, i: (row0 // tq + b * qt + i, h))
    kspec = pl.BlockSpec((L, LANES), lambda b, h, i: (row0 // L + b, h))
    vspec = pl.BlockSpec((L, 2 * LANES), lambda b, h, i: (row0 // L + b, h))
    in_specs = [pl.BlockSpec((8, LANES), lambda b, h, i: (0, 0)), qspec, qspec, kspec, kspec, vspec]
    operands = [prm, qa, qb, ka, kb, vx]
    n_ctx = 0
    if ctx is not None:
        lc = ctx[0].shape[1]
        n_ctx = lc // tk
        in_specs += [pl.BlockSpec((None, lc, LANES), lambda b, h, i: (b, 0, h))] * 2
        in_specs += [pl.BlockSpec((None, lc, 2 * LANES), lambda b, h, i: (b, 0, h))]
        operands += list(ctx)

    def call(online, *ops):
        return pl.pallas_call(
            functools.partial(_attn_kernel, tk=tk, n_ctx=n_ctx, n_self=L // tk, online=online),
            grid=(nb, N_HEADS_A, qt),
            in_specs=in_specs,
            out_specs=pl.BlockSpec((tq, LANES), lambda b, h, i: (b * qt + i, h)),
            out_shape=jax.ShapeDtypeStruct((nb * L, ATTN_W), BF16),
            scratch_shapes=[pltpu.VMEM((2, tq, 2 * LANES), F32), pltpu.VMEM((2, tq, 1), F32)],
            compiler_params=_cparams("parallel", "parallel", "parallel"),
        )(*ops)

    return lax.cond(fast_ok, functools.partial(call, False), functools.partial(call, True), *operands)


def _conv_kernel(zc_ref, zp_ref, zn_ref, cw_ref, cb_ref, lng_ref, lnb_ref, o_ref, u_ref, c_ref, *, rows, tm):
    i = pl.program_id(0)
    local, per = rows.seg_local(i, tm)
    H = CONV_HALO

    def glu(z):
        return z[:, :CONV_W] * jax.nn.sigmoid(z[:, CONV_W:])

    u_ref[H:H + tm, :] = glu(zc_ref[...])
    u_ref[0:H, :] = jnp.where(local > 0, glu(zp_ref[...]), 0.0)
    u_ref[H + tm:2 * H + tm, :] = jnp.where(local < per - 1, glu(zn_ref[...]), 0.0)

    rb = min(tm, 128)

    def strip(s, carry):
        ls = pl.ds(pl.multiple_of(s * LANES, LANES), LANES)
        for r0 in range(0, tm, rb):
            acc = jnp.zeros((rb, LANES), F32)
            for j in range(CONV_K):
                start = r0 + j + H - CONV_K // 2
                acc = acc + u_ref[start:start + rb, ls] * cw_ref[j:j + 1, ls]
            c_ref[r0:r0 + rb, ls] = acc + cb_ref[:, ls]
        return carry

    lax.fori_loop(0, CONV_W // LANES, strip, 0)
    u = c_ref[...]
    mu = jnp.mean(u, axis=-1, keepdims=True)
    d = u - mu
    var = jnp.mean(d * d, axis=-1, keepdims=True)
    y = d * lax.rsqrt(var + EPS) * lng_ref[...] + lnb_ref[...]
    o_ref[...] = _silu(y).astype(o_ref.dtype)


def _conv_branch(z, cw, cb, lng, lnb, rows):
    R = z.shape[0]
    tm = min(256, rows.seq_p)
    H = CONV_HALO
    hb = tm // H
    cblk = OFF_GLU // (2 * CONV_W)
    cwp = jnp.concatenate([cw, jnp.zeros((8 - CONV_K % 8, CONV_W), F32)], axis=0)
    vec = pl.BlockSpec((1, CONV_W), lambda i: (0, 0))
    return pl.pallas_call(
        functools.partial(_conv_kernel, rows=rows, tm=tm),
        grid=(R // tm,),
        in_specs=[pl.BlockSpec((tm, 2 * CONV_W), lambda i: (i, cblk)),
                  pl.BlockSpec((H, 2 * CONV_W), lambda i: (jnp.maximum(i * hb - 1, 0), cblk)),
                  pl.BlockSpec((H, 2 * CONV_W), lambda i: (jnp.minimum((i + 1) * hb, R // H - 1), cblk)),
                  pl.BlockSpec(cwp.shape, lambda i: (0, 0)), vec, vec, vec],
        out_specs=pl.BlockSpec((tm, CONV_W), lambda i: (i, 0)),
        out_shape=jax.ShapeDtypeStruct((R, CONV_W), BF16),
        scratch_shapes=[pltpu.VMEM((tm + 2 * H, CONV_W), F32), pltpu.VMEM((tm, CONV_W), F32)],
        compiler_params=_cparams("parallel"),
    )(z, z, z, cwp, cb.reshape(1, -1), lng.reshape(1, -1), lnb.reshape(1, -1))


def _split3(x):
    a = x.astype(BF16)
    r = x - a.astype(F32)
    b = r.astype(BF16)
    c = (r - b.astype(F32)).astype(BF16)
    return a, b, c


def _hgrn_kernel(zq_ref, zi_ref, zf_ref, lb_ref, s0_ref, o_ref, sf_ref, st_ref, *, rows, reverse, nt):
    T, SUB = REC_TILE, REC_SUB
    t = pl.program_id(1)
    tt = (nt - 1 - t) if reverse else t
    local, per = rows.seg_local(tt, T)
    first = (local == per - 1) if reverse else (local == 0)

    @pl.when(first)
    def _():
        st_ref[...] = s0_ref[...].T

    zq, v, lb = zq_ref[...], zi_ref[...], lb_ref[0]
    q = _silu(zq)
    f = lb + (1.0 - lb) * jax.nn.sigmoid(zf_ref[...])
    k = 1.0 - f
    g = jnp.log(f)
    row = lax.broadcasted_iota(jnp.int32, (T, T), 0)
    col = lax.broadcasted_iota(jnp.int32, (T, T), 1)
    tri = jnp.where((col >= row) if reverse else (col <= row), 1.0, 0.0).astype(BF16)
    g1, g2, g3 = _split3(g)
    b = _dot(tri, g1) + _dot(tri, g2) + _dot(tri, g3)
    btot = jnp.sum(g, axis=0, keepdims=True)
    st = st_ref[...]
    vb = v.astype(BF16)

    o = _nt_dot((q * jnp.exp(b)).astype(BF16), st.astype(BF16))

    rowv = lax.broadcasted_iota(jnp.int32, (T, 1), 0)
    sub_row = lax.broadcasted_iota(jnp.int32, (SUB, 1), 0)
    lane = lax.broadcasted_iota(jnp.int32, (1, T), 1)
    blocks = []
    for i in range(T // SUB):
        lo, hi = SUB * i, SUB * (i + 1)
        bi, qi = b[lo:hi], q[lo:hi]
        has_off = (i < T // SUB - 1) if reverse else (i > 0)
        if has_off:
            r = b[hi:hi + 1] if reverse else b[lo - 1:lo]
            qt = (qi * jnp.exp(bi - r)).astype(BF16)
            kt = k * jnp.exp(jnp.minimum(r - b, 0.0))
            kt = jnp.where((rowv >= hi) if reverse else (rowv < lo), kt, 0.0).astype(BF16)
            a_i = _nt_dot(qt, kt)
        else:
            a_i = jnp.zeros((SUB, T), F32)
        for s in range(SUB):
            rs = lo + s
            e = jnp.exp(jnp.minimum(bi - b[rs:rs + 1], 0.0))
            cv = jnp.sum(qi * e * k[rs:rs + 1], axis=-1, keepdims=True)
            cv = jnp.where((sub_row <= s) if reverse else (sub_row >= s), cv, 0.0)
            a_i = jnp.where(lane == rs, cv, a_i)
        blocks.append(a_i)
    a = jnp.concatenate(blocks, axis=0)
    o_ref[...] = o + _dot(a.astype(BF16), vb)

    kd = (k * jnp.exp(btot - b)).astype(BF16)
    st_new = st * jnp.exp(btot) + _dot(v.T.astype(BF16), kd)
    st_ref[...] = st_new
    sf_ref[...] = st_new.T


def _hgrn_scan(z, lb16, s0, rows, reverse):
    R = z.shape[0]
    T = REC_TILE
    nt = R // T
    d = 1 if reverse else 0
    nseg = rows.nb_p + rows.nb_s
    tile = (lambda t: nt - 1 - t) if reverse else (lambda t: t)
    zblk = lambda off: pl.BlockSpec((T, LANES), lambda h, t: (tile(t), off // LANES + h))
    return pl.pallas_call(
        functools.partial(_hgrn_kernel, rows=rows, reverse=reverse, nt=nt),
        grid=(N_HEADS_R, nt),
        in_specs=[zblk(OFF_RQ), zblk(OFF_RI), zblk(OFF_FB if reverse else OFF_FF),
                  pl.BlockSpec((1, 1, LANES), lambda h, t: (d * N_HEADS_R + h, 0, 0)),
                  pl.BlockSpec((None, None, None, HEAD_DIM_R, HEAD_DIM_R),
                               lambda h, t: (d, rows.seg_index(tile(t), T), h, 0, 0))],
        out_specs=[pl.BlockSpec((T, LANES), lambda h, t: (tile(t), h)),
                   pl.BlockSpec((None, None, HEAD_DIM_R, HEAD_DIM_R),
                                lambda h, t: (rows.seg_index(tile(t), T), h, 0, 0))],
        out_shape=[jax.ShapeDtypeStruct((R, REC_W), F32),
                   jax.ShapeDtypeStruct((nseg, N_HEADS_R, HEAD_DIM_R, HEAD_DIM_R), F32)],
        scratch_shapes=[pltpu.VMEM((HEAD_DIM_R, HEAD_DIM_R), F32)],
        compiler_params=_cparams("parallel", "arbitrary"),
    )(z, z, z, lb16, s0)


def _rec_post_kernel(of_ref, ob_ref, zg_ref, g_ref, o_ref):
    for h in range(N_HEADS_R):
        sl = slice(h * LANES, (h + 1) * LANES)
        o = of_ref[:, sl] + ob_ref[:, sl]
        y = o * lax.rsqrt(jnp.mean(o * o, axis=-1, keepdims=True) + EPS) * g_ref[...]
        o_ref[:, sl] = (y * _silu(zg_ref[:, sl])).astype(o_ref.dtype)


def _rec_post(o_f, o_b, z, g, rows):
    R = z.shape[0]
    tm = min(256, rows.seq_p)
    blk = pl.BlockSpec((tm, REC_W), lambda i: (i, 0))
    return pl.pallas_call(
        _rec_post_kernel,
        grid=(R // tm,),
        in_specs=[blk, blk, pl.BlockSpec((tm, REC_W), lambda i: (i, OFF_RG // REC_W)),
                  pl.BlockSpec((1, LANES), lambda i: (0, 0))],
        out_specs=blk,
        out_shape=jax.ShapeDtypeStruct((R, REC_W), BF16),
        compiler_params=_cparams("parallel"),
    )(o_f, o_b, z, g.reshape(1, LANES))


def _merge_kernel(uap_ref, uas_ref, uc_ref, uh_ref, wa_ref, wc_ref, wh_ref, ga_ref, gc_ref, gh_ref, o_ref, *, npt):
    ua = jnp.where(pl.program_id(1) < npt, uap_ref[...], uas_ref[...])
    m = jax.nn.sigmoid(ga_ref[...]) * _dot(ua, wa_ref[...])
    m = m + jax.nn.sigmoid(gc_ref[...]) * _dot(uc_ref[...], wc_ref[...])
    m = m + jax.nn.sigmoid(gh_ref[...]) * _dot(uh_ref[...], wh_ref[...])
    o_ref[...] = m.astype(o_ref.dtype)


def _merge(ua_p, ua_s, uc, uh, wa, wc, wh, z):
    R = uc.shape[0]
    D = wa.shape[1]
    tm, tn = min(512, ua_p.shape[0]), min(512, D)
    npt = ua_p.shape[0] // tm
    gb = OFF_GATE // tn
    nd = D // tn
    act = lambda w: pl.BlockSpec((tm, w), lambda j, i: (i, 0))
    wgt = lambda w: pl.BlockSpec((w, tn), lambda j, i: (0, j))
    gate = lambda br: pl.BlockSpec((tm, tn), lambda j, i: (i, gb + br * nd + j))
    return pl.pallas_call(
        functools.partial(_merge_kernel, npt=npt),
        grid=(D // tn, R // tm),
        in_specs=[pl.BlockSpec((tm, ATTN_W), lambda j, i: (jnp.minimum(i, npt - 1), 0)),
                  pl.BlockSpec((tm, ATTN_W), lambda j, i: (jnp.maximum(i - npt, 0), 0)),
                  act(CONV_W), act(REC_W), wgt(ATTN_W), wgt(CONV_W), wgt(REC_W),
                  gate(0), gate(1), gate(2)],
        out_specs=pl.BlockSpec((tm, tn), lambda j, i: (i, j)),
        out_shape=jax.ShapeDtypeStruct((R, D), BF16),
        compiler_params=_cparams("parallel", "parallel"),
    )(ua_p, ua_s, uc, uh, wa, wc, wh, z, z, z)


def _router_kernel(w_ref, h_ref, o_ref):
    s = _nt_dot(w_ref[...], h_ref[...])
    e = jnp.exp(s - jnp.max(s, axis=0, keepdims=True))
    o_ref[...] = e / jnp.sum(e, axis=0, keepdims=True)


def _router(h, w_router):
    R, D = h.shape
    tm = min(512, R)
    wt = w_router.T.astype(BF16)
    return pl.pallas_call(
        _router_kernel,
        grid=(R // tm,),
        in_specs=[pl.BlockSpec((N_EXPERTS, D), lambda i: (0, 0)),
                  pl.BlockSpec((tm, D), lambda i: (i, 0))],
        out_specs=pl.BlockSpec((N_EXPERTS, tm), lambda i: (0, i)),
        out_shape=jax.ShapeDtypeStruct((N_EXPERTS, R), F32),
        compiler_params=_cparams("parallel"),
    )(wt, h)


def _topk_kernel(a_ref, slot_ref, gate_ref, *, n, cap, blocks_per_route):
    x = a_ref[...]
    bits = pltpu.bitcast(x, jnp.int32)

    def search(i, thr):
        cand = thr | (1 << (30 - i))
        cnt = jnp.sum(jnp.where(bits >= cand, 1.0, 0.0), axis=-1, keepdims=True)
        return jnp.where(cnt >= cap, cand, thr)

    thr = lax.fori_loop(0, 31, search, jnp.zeros((N_EXPERTS, 1), jnp.int32))
    gt = bits > thr
    eq = bits == thr
    need = cap - jnp.sum(jnp.where(gt, 1.0, 0.0), axis=-1, keepdims=True)

    cb = min(n, 512)
    ci = lax.broadcasted_iota(jnp.int32, (cb, cb), 0)
    cj = lax.broadcasted_iota(jnp.int32, (cb, cb), 1)
    upper = jnp.where(ci <= cj, 1.0, 0.0).astype(BF16)

    def prefix(mask01):
        outs, run = [], jnp.zeros((N_EXPERTS, 1), F32)
        for c in range(n // cb):
            inc = _dot(mask01[:, c * cb:(c + 1) * cb].astype(BF16), upper) + run
            outs.append(inc)
            run = inc[:, cb - 1:cb]
        return jnp.concatenate(outs, axis=1) if len(outs) > 1 else outs[0]

    eq01 = jnp.where(eq, 1.0, 0.0)
    sel = gt | (eq & (prefix(eq01) - eq01 < need))
    sel01 = jnp.where(sel, 1.0, 0.0)
    pos = prefix(sel01) - sel01
    base = (pl.program_id(0) % blocks_per_route) * cap
    slot_ref[...] = jnp.where(sel, pos.astype(jnp.int32) + base, -1)
    gate_ref[...] = jnp.where(sel, x, 0.0)


def _topk(aff_t, n, cap, col0, ncols, blocks_per_route):
    nb = ncols // n
    off = col0 // n
    blk_in = pl.BlockSpec((N_EXPERTS, n), lambda s: (0, off + s))
    blk = pl.BlockSpec((N_EXPERTS, n), lambda s: (0, s))
    return pl.pallas_call(
        functools.partial(_topk_kernel, n=n, cap=cap, blocks_per_route=blocks_per_route),
        grid=(nb,),
        in_specs=[blk_in],
        out_specs=[blk, blk],
        out_shape=[jax.ShapeDtypeStruct((N_EXPERTS, ncols), jnp.int32),
                   jax.ShapeDtypeStruct((N_EXPERTS, ncols), F32)],
        compiler_params=_cparams("parallel"),
    )(aff_t)


def _gather_kernel(slot_ref, h_ref, x_ref, acc_ref, *, cap):
    e, kt = pl.program_id(1), pl.program_id(2)

    @pl.when(kt == 0)
    def _():
        acc_ref[...] = jnp.zeros(acc_ref.shape, F32)

    slots = slot_ref[pl.ds(e, 1), :]
    p = lax.broadcasted_iota(jnp.int32, (cap, 1), 0)
    onehot = jnp.where(slots == p, 1.0, 0.0).astype(BF16)
    acc_ref[...] += _dot(onehot, h_ref[...])

    @pl.when(kt == pl.num_programs(2) - 1)
    def _():
        x_ref[0] = acc_ref[...].astype(x_ref.dtype)


def _gather(slot, h, n, cap):
    R, D = h.shape
    nseg = R // n
    tk = min(512, n)
    return pl.pallas_call(
        functools.partial(_gather_kernel, cap=cap),
        grid=(nseg, N_EXPERTS, n // tk),
        in_specs=[pl.BlockSpec((N_EXPERTS, tk), lambda s, e, k: (0, s * (n // tk) + k)),
                  pl.BlockSpec((tk, D), lambda s, e, k: (s * (n // tk) + k, 0))],
        out_specs=pl.BlockSpec((1, cap, D), lambda s, e, k: (e, s, 0)),
        out_shape=jax.ShapeDtypeStruct((N_EXPERTS, nseg * cap, D), BF16),
        scratch_shapes=[pltpu.VMEM((cap, D), F32)],
        compiler_params=_cparams("parallel", "parallel", "arbitrary"),
    )(slot, h)


def _ffn_up_kernel(x_ref, wg_ref, wu_ref, o_ref):
    x = x_ref[0]
    a = _dot(x, wg_ref[...].astype(BF16))
    b = _dot(x, wu_ref[...].astype(BF16))
    o_ref[0] = (_silu(a) * b).astype(o_ref.dtype)


def _ffn_up(x, wg, wu, l):
    E, M, D = x.shape
    Fh = wg.shape[3]
    tm = M // 2 if (M // 2) % 16 == 0 else M
    tf = min(256, Fh)
    wblk = pl.BlockSpec((None, None, D, tf), lambda e, i, j: (l, e, 0, j))
    return pl.pallas_call(
        _ffn_up_kernel,
        grid=(E, M // tm, Fh // tf),
        in_specs=[pl.BlockSpec((1, tm, D), lambda e, i, j: (e, i, 0)), wblk, wblk],
        out_specs=pl.BlockSpec((1, tm, tf), lambda e, i, j: (e, i, j)),
        out_shape=jax.ShapeDtypeStruct((E, M, Fh), BF16),
        compiler_params=_cparams("parallel", "parallel", "parallel"),
    )(x, wg, wu)


def _ffn_down_kernel(h_ref, w_ref, o_ref):
    o_ref[0] = _dot(h_ref[0], w_ref[...].astype(BF16)).astype(o_ref.dtype)


def _ffn_down(hid, wd, l):
    E, M, Fh = hid.shape
    D = wd.shape[3]
    tm = M // 2 if (M // 2) % 16 == 0 else M
    tn = min(1024, D)
    return pl.pallas_call(
        _ffn_down_kernel,
        grid=(E, M // tm, D // tn),
        in_specs=[pl.BlockSpec((1, tm, Fh), lambda e, i, j: (e, i, 0)),
                  pl.BlockSpec((None, None, Fh, tn), lambda e, i, j: (l, e, 0, j))],
        out_specs=pl.BlockSpec((1, tm, tn), lambda e, i, j: (e, i, j)),
        out_shape=jax.ShapeDtypeStruct((E, M, D), BF16),
        compiler_params=_cparams("parallel", "parallel", "parallel"),
    )(hid, wd)


def _combine_kernel(slot_ref, gate_ref, y_ref, x_ref, g_ref, o_ref, acc_ref, *, cap):
    e = pl.program_id(2)

    @pl.when(e == 0)
    def _():
        acc_ref[...] = jnp.zeros(acc_ref.shape, F32)

    lane_e = lax.broadcasted_iota(jnp.int32, (1, N_EXPERTS), 1) == e
    slot = jnp.sum(jnp.where(lane_e, slot_ref[...], 0.0), axis=-1, keepdims=True)
    gate = jnp.sum(jnp.where(lane_e, gate_ref[...], 0.0), axis=-1, keepdims=True)
    p = lax.broadcasted_iota(jnp.int32, (1, cap), 1).astype(F32)
    onehot = jnp.where(slot == p, 1.0, 0.0).astype(BF16)
    acc_ref[...] += gate * _dot(onehot, y_ref[0])

    @pl.when(e == pl.num_programs(2) - 1)
    def _():
        o_ref[...] = x_ref[...] + g_ref[0] * acc_ref[...]


def _combine(slot_t, gate_t, y, x, mod, rows, i_gate, n, cap):
    R, D = x.shape
    T = min(256, n)
    per = n // T
    tok = lambda w: pl.BlockSpec((T, w), lambda s, i, e: (s * per + i, 0))
    return pl.pallas_call(
        functools.partial(_combine_kernel, cap=cap),
        grid=(R // n, per, N_EXPERTS),
        in_specs=[tok(N_EXPERTS), tok(N_EXPERTS),
                  pl.BlockSpec((1, cap, D), lambda s, i, e: (e, s, 0)),
                  tok(D),
                  pl.BlockSpec((1, 1, D), lambda s, i, e: (rows.mod_row(s, n) * N_MOD + i_gate, 0, 0))],
        out_specs=tok(D),
        out_shape=jax.ShapeDtypeStruct((R, D), F32),
        scratch_shapes=[pltpu.VMEM((T, D), F32)],
        compiler_params=_cparams("parallel", "parallel", "arbitrary"),
    )(slot_t, gate_t, y, x, mod)


def _moe(h2, x, mod, rows, l, w_router, w_gate, w_up, w_down):
    P, S = rows.P, rows.seq_s
    aff_t = _router(h2, w_router[l])
    cap_p = CAP_FACTOR * rows.seq_p // N_EXPERTS
    cap_s = CAP_FACTOR * S // N_EXPERTS
    slot_p, gate_p = _topk(aff_t, rows.seq_p, cap_p, 0, P, rows.nb_p)
    slot_s, gate_s = _topk(aff_t, S, cap_s, P, rows.R - P, 1)
    slot = jnp.concatenate([slot_p, slot_s], axis=1)
    gate = jnp.concatenate([gate_p, gate_s], axis=1)
    xs = _gather(slot, h2, S, cap_s)
    ys = _ffn_down(_ffn_up(xs, w_gate, w_up, l), w_down, l)
    return _combine(slot.T.astype(F32), gate.T, ys, x, mod, rows, 5, S, cap_s)


def _layer(x, l, rows, mod, p, tables, lam_val, lam_init, lbs, cache_k, cache_v, state_hgrn):
    R, D = x.shape
    P = rows.P
    h = _modulate(x, p["norm1_g"][l], mod, rows, 0, 1)
    z = _linear(h, p["w_in"], l, 1024, 512, F32)

    gq, gk = p["qk_norm_g"][l, 0], p["qk_norm_g"][l, 1]
    tile2 = lambda g: jnp.tile(g, 2).reshape(1, LANES)
    offset = 1.02 * HEAD_DIM_A * Q_SCALE * jnp.max(jnp.abs(gq)) * jnp.max(jnp.abs(gk))
    pad = jnp.zeros((8, LANES), F32).at[0, HEAD_DIM_A].set(-offset).at[1, HEAD_DIM_A].set(1.0)
    qa, qb, ka, kb, k_f, vx = _qkv_prep(z, tile2(gq), tile2(gk), pad, tables[0], tables[1], rows)
    prm = jnp.zeros((8, LANES), F32)
    prm = prm.at[0].set(p["attn_norm_g"][l]).at[1].set(lam_val).at[2].set(1.0 - lam_init)
    fast_ok = offset <= FAST_SOFTMAX_MAX_OFFSET
    ck = cache_k[:, l]
    kpad = jnp.broadcast_to(pad[1, HEAD_DIM_A:], ck.shape[:3] + (HEAD_DIM_A,))
    ctx_k = [jnp.concatenate([ck[:, :, :, m], kpad], axis=-1).reshape(rows.nb_s, -1, ATTN_W).astype(BF16)
             for m in range(2)]
    cv = cache_v[:, l]
    ctx_v = jnp.concatenate([cv, jnp.ones_like(cv)], axis=-1).reshape(rows.nb_s, -1, 2 * ATTN_W).astype(BF16)
    ua_p = _attention(prm, qa, qb, ka, kb, vx, None, 0, rows.nb_p, rows.seq_p, 256, 256, fast_ok)
    ua_s = _attention(prm, qa, qb, ka, kb, vx, (ctx_k[0], ctx_k[1], ctx_v), P, rows.nb_s, rows.seq_s,
                      512, 512, fast_ok)

    uc = _conv_branch(z, p["conv_w"][l], p["conv_b"][l], p["conv_ln_g"][l], p["conv_ln_b"][l], rows)

    s0 = jnp.concatenate([jnp.zeros((2, rows.nb_p, N_HEADS_R, HEAD_DIM_R, HEAD_DIM_R), F32),
                          jnp.swapaxes(state_hgrn[:, l], 0, 1).astype(F32)], axis=1)
    lb16 = lbs[l].reshape(2 * N_HEADS_R, 1, HEAD_DIM_R)
    o_f, s_f = _hgrn_scan(z, lb16, s0, rows, False)
    o_b, s_b = _hgrn_scan(z, lb16, s0, rows, True)
    uh = _rec_post(o_f, o_b, z, p["hgrn_norm_g"][l], rows)

    bf = lambda w: w[l].astype(BF16)
    m = _merge(ua_p, ua_s, uc, uh, bf(p["w_attn_o"]), bf(p["w_conv_o"]), bf(p["w_hgrn_o"]), z)
    x = _linear_res(m, bf(p["w_out"]), x, mod, rows, 2, 512, 1024)

    h2 = _modulate(x, p["norm2_g"][l], mod, rows, 3, 4)
    x = _moe(h2, x, mod, rows, l, p["w_router"], p["w_e_gate"], p["w_e_up"], p["w_e_down"])

    new_k = k_f[:P].reshape(rows.nb_p, rows.seq_p, N_HEADS_A, 2, HEAD_DIM_A)
    new_v = z[:P, OFF_V:OFF_V + ATTN_W].reshape(rows.nb_p, rows.seq_p, N_HEADS_A, 2 * HEAD_DIM_A)
    new_s = jnp.stack([s_f[:rows.nb_p], s_b[:rows.nb_p]], axis=1)
    return x, new_k, new_v, new_s


def _rope_tables(n_tok, tm):
    t = jnp.arange(n_tok)
    row = (t // GRID_W).astype(F32)
    col = (t % GRID_W).astype(F32)
    inv = ROPE_BASE ** (-jnp.arange(0, AXIS_DIM, 2, dtype=F32) / AXIS_DIM)
    ar, ac = row[:, None] * inv, col[:, None] * inv
    ang = jnp.concatenate([ar, ar, ac, ac], axis=-1)
    sign = jnp.where((jnp.arange(HEAD_DIM_A) & (AXIS_DIM // 2)) != 0, 1.0, -1.0).astype(F32)
    cos = jnp.tile(jnp.cos(ang), (1, 2))
    sin = jnp.tile(jnp.sin(ang) * sign, (1, 2))
    cos = jnp.concatenate([jnp.ones((tm, LANES), F32), cos], axis=0)
    sin = jnp.concatenate([jnp.zeros((tm, LANES), F32), sin], axis=0)
    return cos, sin


def kernel(x_prompt, x_sample, cache_k, cache_v, state_hgrn, c, c_ctx, norm1_g, norm2_g, w_ada, b_ada, w_in, qk_norm_g, lam, attn_norm_g, w_attn_o, conv_w, conv_b, conv_ln_g, conv_ln_b, w_conv_o, hgrn_lb, hgrn_norm_g, w_hgrn_o, w_out, w_router, w_e_gate, w_e_up, w_e_down):
    nb_p, seq_p, D = x_prompt.shape
    nb_s, seq_s, _ = x_sample.shape
    depth = w_in.shape[0]
    rows = _Rows(nb_p, seq_p, nb_s, seq_s)
    p = dict(norm1_g=norm1_g, norm2_g=norm2_g, w_in=w_in, qk_norm_g=qk_norm_g, attn_norm_g=attn_norm_g,
             w_attn_o=w_attn_o, conv_w=conv_w, conv_b=conv_b, conv_ln_g=conv_ln_g, conv_ln_b=conv_ln_b,
             w_conv_o=w_conv_o, hgrn_norm_g=hgrn_norm_g, w_hgrn_o=w_hgrn_o, w_out=w_out, w_router=w_router,
             w_e_gate=w_e_gate, w_e_up=w_e_up, w_e_down=w_e_down)

    sm = jax.nn.softmax(hgrn_lb.astype(F32), axis=0)
    lbs = jnp.cumsum(sm, axis=0) - sm[0]
    tables = _rope_tables(seq_s, min(256, seq_p))
    cond8 = jnp.concatenate([c_ctx[None], c, jnp.zeros((8 - 1 - nb_s, D), F32)], axis=0)

    x = jnp.concatenate([x_prompt.reshape(rows.P, D), x_sample.reshape(nb_s * seq_s, D)], axis=0)
    new_k, new_v, new_s = [], [], []
    for l in range(depth):
        lam_init = 0.8 - 0.6 * math.exp(-0.3 * l)
        lam_l = lam[l].astype(F32)
        lam_val = jnp.exp(jnp.sum(lam_l[0] * lam_l[1])) - jnp.exp(jnp.sum(lam_l[2] * lam_l[3])) + lam_init
        mod = _ada(cond8, w_ada, b_ada, l).reshape(8 * N_MOD, 1, D)
        x, k_l, v_l, s_l = _layer(x, l, rows, mod, p, tables, lam_val, lam_init, lbs, cache_k, cache_v, state_hgrn)
        new_k.append(k_l)
        new_v.append(v_l)
        new_s.append(s_l)
    y_prompt = x[:rows.P].reshape(nb_p, seq_p, D)
    y_sample = x[rows.P:].reshape(nb_s, seq_s, D)
    return (y_prompt, y_sample, jnp.stack(new_k, axis=1), jnp.stack(new_v, axis=1), jnp.stack(new_s, axis=1))
```

```python
import functools
import math

import jax
import jax.numpy as jnp
from jax import lax
from jax.experimental import pallas as pl
from jax.experimental.pallas import tpu as pltpu

F32 = jnp.float32
BF16 = jnp.bfloat16

GRID_W = 64
N_HEADS_A = 16
HEAD_DIM_A = 64
ATTN_W = N_HEADS_A * 2 * HEAD_DIM_A
AXIS_DIM = HEAD_DIM_A // 2
ROPE_BASE = 10000.0
CONV_W = 1024
CONV_K = 31
N_HEADS_R = 8
HEAD_DIM_R = 128
REC_W = N_HEADS_R * HEAD_DIM_R
N_EXPERTS = 16
CAP_FACTOR = 2
N_BRANCH = 3
N_MOD = 6
EPS = 1e-6

LANES = 128
REC_TILE = 128
REC_SUB = 16
CONV_HALO = 16
VMEM_LIMIT = 56 * 1024 * 1024

OFF_Q = 0
OFF_K = ATTN_W
OFF_V = 2 * ATTN_W
OFF_GLU = 3 * ATTN_W
OFF_RQ = OFF_GLU + 2 * CONV_W
OFF_RI = OFF_RQ + REC_W
OFF_FF = OFF_RI + REC_W
OFF_FB = OFF_FF + REC_W
OFF_RG = OFF_FB + REC_W
OFF_GATE = OFF_RG + REC_W


def _cparams(*sem):
    return pltpu.CompilerParams(dimension_semantics=sem, vmem_limit_bytes=VMEM_LIMIT)


def _nt_dot(a, b):
    return lax.dot_general(a, b, (((1,), (1,)), ((), ())), preferred_element_type=F32)


def _dot(a, b):
    return jnp.dot(a, b, preferred_element_type=F32)


def _silu(x):
    return x * jax.nn.sigmoid(x)


class _Rows:
    def __init__(self, nb_p, seq_p, nb_s, seq_s):
        self.nb_p, self.seq_p, self.nb_s, self.seq_s = nb_p, seq_p, nb_s, seq_s
        self.P = nb_p * seq_p
        self.R = self.P + nb_s * seq_s
        assert self.P == seq_s, "prompt rows are routed as one sample-sized block"

    def mod_row(self, i, tm):
        r = i * tm
        return jnp.where(r < self.P, 0, 1 + (r - self.P) // self.seq_s)

    def seg_local(self, i, tm):
        tp, ts, npt = self.seq_p // tm, self.seq_s // tm, self.P // tm
        in_p = i < npt
        local = jnp.where(in_p, i % tp, (i - npt) % ts)
        per = jnp.where(in_p, tp, ts)
        return local, per

    def seg_index(self, i, tm):
        tp, ts, npt = self.seq_p // tm, self.seq_s // tm, self.P // tm
        return jnp.where(i < npt, i // tp, self.nb_p + (i - npt) // ts)


def _ada_kernel(c_ref, w_ref, b_ref, o_ref):
    a = _silu(c_ref[...]).astype(BF16)
    o_ref[...] = _dot(a, w_ref[...].astype(BF16)) + b_ref[...]


def _ada(cond8, w, b, l):
    _, D, N = w.shape
    tn = min(N, 512)
    return pl.pallas_call(
        _ada_kernel,
        grid=(N // tn,),
        in_specs=[pl.BlockSpec((8, D), lambda j: (0, 0)),
                  pl.BlockSpec((None, D, tn), lambda j: (l, 0, j)),
                  pl.BlockSpec((None, 1, tn), lambda j: (l, 0, j))],
        out_specs=pl.BlockSpec((8, tn), lambda j: (0, j)),
        out_shape=jax.ShapeDtypeStruct((8, N), F32),
        compiler_params=_cparams("parallel"),
    )(cond8, w, b.reshape(b.shape[0], 1, N))


def _modulate_kernel(x_ref, g_ref, sh_ref, sc_ref, o_ref):
    x = x_ref[...]
    y = x * lax.rsqrt(jnp.mean(x * x, axis=-1, keepdims=True) + EPS) * g_ref[...]
    o_ref[...] = (y * (1.0 + sc_ref[0]) + sh_ref[0]).astype(o_ref.dtype)


def _modulate(x, g, mod, rows, i_shift, i_scale, out_dtype):
    R, D = x.shape
    tm = min(256, rows.seq_p)
    return pl.pallas_call(
        _modulate_kernel,
        grid=(R // tm,),
        in_specs=[pl.BlockSpec((tm, D), lambda i: (i, 0)),
                  pl.BlockSpec((1, D), lambda i: (0, 0)),
                  pl.BlockSpec((1, 1, D), lambda i: (rows.mod_row(i, tm) * N_MOD + i_shift, 0, 0)),
                  pl.BlockSpec((1, 1, D), lambda i: (rows.mod_row(i, tm) * N_MOD + i_scale, 0, 0))],
        out_specs=pl.BlockSpec((tm, D), lambda i: (i, 0)),
        out_shape=jax.ShapeDtypeStruct((R, D), out_dtype),
        compiler_params=_cparams("parallel"),
    )(x, g.reshape(1, D), mod, mod)


def _linear_kernel(a_ref, w_ref, o_ref):
    o_ref[...] = _dot(a_ref[...], w_ref[...].astype(BF16)).astype(o_ref.dtype)


def _linear(a, w, l, tm, tn, out_dtype):
    M, K = a.shape
    N = w.shape[2]
    tm, tn = min(tm, M), min(tn, N)
    return pl.pallas_call(
        _linear_kernel,
        grid=(M // tm, N // tn),
        in_specs=[pl.BlockSpec((tm, K), lambda i, j: (i, 0)),
                  pl.BlockSpec((None, K, tn), lambda i, j: (l, 0, j))],
        out_specs=pl.BlockSpec((tm, tn), lambda i, j: (i, j)),
        out_shape=jax.ShapeDtypeStruct((M, N), out_dtype),
        compiler_params=_cparams("parallel", "parallel"),
    )(a, w)


def _linear_res_kernel(a_ref, w_ref, x_ref, g_ref, o_ref):
    o_ref[...] = x_ref[...] + g_ref[0] * _dot(a_ref[...], w_ref[...])


def _linear_res(a, w, x, mod, rows, i_gate, tm, tn):
    M, K = a.shape
    N = w.shape[1]
    tm, tn = min(tm, rows.P), min(tn, N)
    return pl.pallas_call(
        _linear_res_kernel,
        grid=(N // tn, M // tm),
        in_specs=[pl.BlockSpec((tm, K), lambda j, i: (i, 0)),
                  pl.BlockSpec((K, tn), lambda j, i: (0, j)),
                  pl.BlockSpec((tm, tn), lambda j, i: (i, j)),
                  pl.BlockSpec((1, 1, tn), lambda j, i: (rows.mod_row(i, tm) * N_MOD + i_gate, 0, j))],
        out_specs=pl.BlockSpec((tm, tn), lambda j, i: (i, j)),
        out_shape=jax.ShapeDtypeStruct((M, N), F32),
        compiler_params=_cparams("parallel", "parallel"),
    )(a, w, x, mod)


LOG2E = math.log2(math.e)
Q_SCALE = HEAD_DIM_A ** -0.5 * LOG2E
FAST_SOFTMAX_MAX_OFFSET = 55.0


def _qkv_kernel(zq_ref, zk_ref, zv_ref, gq_ref, gk_ref, pad_ref, cos_ref, sin_ref,
                qa_ref, qb_ref, ka_ref, kb_ref, kf_ref, vx_ref):
    cos, sin = cos_ref[...], sin_ref[...]
    lane = lax.broadcasted_iota(jnp.int32, (1, LANES), 1)
    upper = (lane & (AXIS_DIM // 2)) != 0
    lower_map = lane < HEAD_DIM_A
    gi = lax.broadcasted_iota(jnp.int32, (LANES, LANES), 0) // HEAD_DIM_A
    gj = lax.broadcasted_iota(jnp.int32, (LANES, LANES), 1) // HEAD_DIM_A
    group = jnp.where(gi == gj, 1.0, 0.0).astype(BF16)
    qpad, kpad = pad_ref[0:1, :], pad_ref[1:2, :]
    ones = jnp.ones((zq_ref.shape[0], LANES), BF16)

    def norm_rope(x, g):
        x2 = x * x
        hi = x2.astype(BF16)
        lo = (x2 - hi.astype(F32)).astype(BF16)
        ss = _dot(hi, group) + _dot(lo, group)
        y = x * lax.rsqrt(ss * (1.0 / HEAD_DIM_A) + EPS) * g
        half = AXIS_DIM // 2
        rot = jnp.where(upper, pltpu.roll(y, half, 1), pltpu.roll(y, LANES - half, 1))
        return y * cos + rot * sin

    def split_maps(y, pad, a_ref, b_ref, sl):
        a_ref[:, sl] = jnp.where(lower_map, y, pad).astype(BF16)
        b_ref[:, sl] = jnp.where(lower_map, pltpu.roll(y, HEAD_DIM_A, 1), pad).astype(BF16)

    for h in range(N_HEADS_A):
        sl = slice(h * LANES, (h + 1) * LANES)
        q = norm_rope(zq_ref[:, sl], gq_ref[...]) * Q_SCALE
        split_maps(q, qpad, qa_ref, qb_ref, sl)
        k = norm_rope(zk_ref[:, sl], gk_ref[...])
        kf_ref[:, sl] = k
        split_maps(k, kpad, ka_ref, kb_ref, sl)
        vx_ref[:, 2 * h * LANES:(2 * h + 1) * LANES] = zv_ref[:, sl].astype(BF16)
        vx_ref[:, (2 * h + 1) * LANES:(2 * h + 2) * LANES] = ones


def _qkv_prep(z, gq, gk, pad, cos_t, sin_t, rows):
    R = z.shape[0]
    tm = min(256, rows.seq_p)
    npt, ts = rows.P // tm, rows.seq_s // tm
    tab = lambda i: (jnp.where(i < npt, 0, 1 + (i - npt) % ts), 0)
    blk = lambda c: pl.BlockSpec((tm, ATTN_W), lambda i: (i, c))
    bf = jax.ShapeDtypeStruct((R, ATTN_W), BF16)
    return pl.pallas_call(
        _qkv_kernel,
        grid=(R // tm,),
        in_specs=[blk(0), blk(1), blk(2),
                  pl.BlockSpec((1, LANES), lambda i: (0, 0)),
                  pl.BlockSpec((1, LANES), lambda i: (0, 0)),
                  pl.BlockSpec((8, LANES), lambda i: (0, 0)),
                  pl.BlockSpec((tm, LANES), tab),
                  pl.BlockSpec((tm, LANES), tab)],
        out_specs=[blk(0)] * 5 + [pl.BlockSpec((tm, 2 * ATTN_W), lambda i: (i, 0))],
        out_shape=[bf, bf, bf, bf, jax.ShapeDtypeStruct((R, ATTN_W), F32),
                   jax.ShapeDtypeStruct((R, 2 * ATTN_W), BF16)],
        compiler_params=_cparams("parallel"),
    )(z, z, z, gq, gk, pad, cos_t, sin_t)


def _attn_kernel(prm_ref, qa_ref, qb_ref, ka_ref, kb_ref, vx_ref, *rest, tk, n_ctx, n_self, online):
    if n_ctx:
        kac_ref, kbc_ref, vxc_ref, o_ref, acc_ref, m_ref = rest
    else:
        o_ref, acc_ref, m_ref = rest
    acc_ref[...] = jnp.zeros(acc_ref.shape, F32)
    if online:
        m_ref[...] = jnp.full(m_ref.shape, -jnp.inf, F32)
    qs = (qa_ref[...], qb_ref[...])

    def chunk(k_refs, v_ref, j):
        ks = pl.ds(pl.multiple_of(j * tk, tk), tk)
        v = v_ref[ks, :]
        for mi in range(2):
            s = _nt_dot(qs[mi], k_refs[mi][ks, :])
            if online:
                m_old = m_ref[mi]
                m_new = jnp.maximum(m_old, jnp.max(s, axis=-1, keepdims=True))
                p = jnp.exp2(s - m_new).astype(BF16)
                acc_ref[mi] = jnp.exp2(m_old - m_new) * acc_ref[mi] + _dot(p, v)
                m_ref[mi] = m_new
            else:
                acc_ref[mi] += _dot(jnp.exp2(s).astype(BF16), v)

    def loop(n, k_refs, v_ref):
        def body(j, carry):
            chunk(k_refs, v_ref, j)
            return carry
        lax.fori_loop(0, n, body, 0)

    if n_ctx:
        loop(n_ctx, (kac_ref, kbc_ref), vxc_ref)
    loop(n_self, (ka_ref, kb_ref), vx_ref)

    g, lam, post = prm_ref[0:1, :], prm_ref[1:2, :], prm_ref[2:3, :]
    a0, a1 = acc_ref[0], acc_ref[1]
    o = a0[:, :LANES] / a0[:, LANES:] - lam * (a1[:, :LANES] / a1[:, LANES:])
    y = o * lax.rsqrt(jnp.mean(o * o, axis=-1, keepdims=True) + EPS) * g
    o_ref[...] = (y * post).astype(o_ref.dtype)


def _attention(prm, qa, qb, ka, kb, vx, ctx, row0, nb, L, tq, tk, fast_ok):
    tq, tk = min(tq, L), min(tk, L)
    qt = L // tq
    qspec = pl.BlockSpec((tq, LANES), lambda b, h, i: (row0 // tq + b * qt + i, h))
    kspec = pl.BlockSpec((L, LANES), lambda b, h, i: (row0 // L + b, h))
    vspec = pl.BlockSpec((L, 2 * LANES), lambda b, h, i: (row0 // L + b, h))
    in_specs = [pl.BlockSpec((8, LANES), lambda b, h, i: (0, 0)), qspec, qspec, kspec, kspec, vspec]
    operands = [prm, qa, qb, ka, kb, vx]
    n_ctx = 0
    if ctx is not None:
        lc = ctx[0].shape[1]
        n_ctx = lc // tk
        in_specs += [pl.BlockSpec((None, lc, LANES), lambda b, h, i: (b, 0, h))] * 2
        in_specs += [pl.BlockSpec((None, lc, 2 * LANES), lambda b, h, i: (b, 0, h))]
        operands += list(ctx)

    def call(online, *ops):
        return pl.pallas_call(
            functools.partial(_attn_kernel, tk=tk, n_ctx=n_ctx, n_self=L // tk, online=online),
            grid=(nb, N_HEADS_A, qt),
            in_specs=in_specs,
            out_specs=pl.BlockSpec((tq, LANES), lambda b, h, i: (b * qt + i, h)),
            out_shape=jax.ShapeDtypeStruct((nb * L, ATTN_W), BF16),
            scratch_shapes=[pltpu.VMEM((2, tq, 2 * LANES), F32), pltpu.VMEM((2, tq, 1), F32)],
            compiler_params=_cparams("parallel", "parallel", "parallel"),
        )(*ops)

    return lax.cond(fast_ok, functools.partial(call, False), functools.partial(call, True), *operands)


def _conv_kernel(zc_ref, zp_ref, zn_ref, cw_ref, cb_ref, lng_ref, lnb_ref, o_ref, u_ref, c_ref, *, rows, tm):
    i = pl.program_id(0)
    local, per = rows.seg_local(i, tm)
    H = CONV_HALO

    def glu(z):
        return z[:, :CONV_W] * jax.nn.sigmoid(z[:, CONV_W:])

    u_ref[H:H + tm, :] = glu(zc_ref[...])
    u_ref[0:H, :] = jnp.where(local > 0, glu(zp_ref[...]), 0.0)
    u_ref[H + tm:2 * H + tm, :] = jnp.where(local < per - 1, glu(zn_ref[...]), 0.0)

    rb = min(tm, 128)

    def strip(s, carry):
        ls = pl.ds(pl.multiple_of(s * LANES, LANES), LANES)
        for r0 in range(0, tm, rb):
            acc = jnp.zeros((rb, LANES), F32)
            for j in range(CONV_K):
                start = r0 + j + H - CONV_K // 2
                acc = acc + u_ref[start:start + rb, ls] * cw_ref[j:j + 1, ls]
            c_ref[r0:r0 + rb, ls] = acc + cb_ref[:, ls]
        return carry

    lax.fori_loop(0, CONV_W // LANES, strip, 0)
    u = c_ref[...]
    mu = jnp.mean(u, axis=-1, keepdims=True)
    d = u - mu
    var = jnp.mean(d * d, axis=-1, keepdims=True)
    y = d * lax.rsqrt(var + EPS) * lng_ref[...] + lnb_ref[...]
    o_ref[...] = _silu(y).astype(o_ref.dtype)


def _conv_branch(z, cw, cb, lng, lnb, rows):
    R = z.shape[0]
    tm = min(256, rows.seq_p)
    H = CONV_HALO
    hb = tm // H
    cblk = OFF_GLU // (2 * CONV_W)
    cwp = jnp.concatenate([cw, jnp.zeros((8 - CONV_K % 8, CONV_W), F32)], axis=0)
    vec = pl.BlockSpec((1, CONV_W), lambda i: (0, 0))
    return pl.pallas_call(
        functools.partial(_conv_kernel, rows=rows, tm=tm),
        grid=(R // tm,),
        in_specs=[pl.BlockSpec((tm, 2 * CONV_W), lambda i: (i, cblk)),
                  pl.BlockSpec((H, 2 * CONV_W), lambda i: (jnp.maximum(i * hb - 1, 0), cblk)),
                  pl.BlockSpec((H, 2 * CONV_W), lambda i: (jnp.minimum((i + 1) * hb, R // H - 1), cblk)),
                  pl.BlockSpec(cwp.shape, lambda i: (0, 0)), vec, vec, vec],
        out_specs=pl.BlockSpec((tm, CONV_W), lambda i: (i, 0)),
        out_shape=jax.ShapeDtypeStruct((R, CONV_W), BF16),
        scratch_shapes=[pltpu.VMEM((tm + 2 * H, CONV_W), F32), pltpu.VMEM((tm, CONV_W), F32)],
        compiler_params=_cparams("parallel"),
    )(z, z, z, cwp, cb.reshape(1, -1), lng.reshape(1, -1), lnb.reshape(1, -1))


def _split3(x):
    a = x.astype(BF16)
    r = x - a.astype(F32)
    b = r.astype(BF16)
    c = (r - b.astype(F32)).astype(BF16)
    return a, b, c


def _hgrn_kernel(*refs, rows, nt):
    fwd, bwd = refs[0:5] + refs[10:12] + refs[14:15], refs[5:10] + refs[12:14] + refs[15:16]
    _hgrn_load_state(fwd[4], fwd[7], rows=rows, reverse=False, nt=nt)
    _hgrn_load_state(bwd[4], bwd[7], rows=rows, reverse=True, nt=nt)
    _hgrn_step(*fwd, reverse=False)
    _hgrn_step(*bwd, reverse=True)


def _hgrn_load_state(s0_ref, st_ref, *, rows, reverse, nt):
    t = pl.program_id(1)
    local, per = rows.seg_local((nt - 1 - t) if reverse else t, REC_TILE)

    @pl.when((local == per - 1) if reverse else (local == 0))
    def _():
        st_ref[...] = s0_ref[...].T


def _hgrn_step(zq_ref, zi_ref, zf_ref, lb_ref, s0_ref, o_ref, sf_ref, st_ref, *, reverse):
    T, SUB = REC_TILE, REC_SUB
    zq, v, lb = zq_ref[...], zi_ref[...], lb_ref[0]
    q = _silu(zq)
    f = lb + (1.0 - lb) * jax.nn.sigmoid(zf_ref[...])
    k = 1.0 - f
    g = jnp.log(f)
    row = lax.broadcasted_iota(jnp.int32, (T, T), 0)
    col = lax.broadcasted_iota(jnp.int32, (T, T), 1)
    tri = jnp.where((col >= row) if reverse else (col <= row), 1.0, 0.0).astype(BF16)
    g1, g2, g3 = _split3(g)
    b = _dot(tri, g1) + _dot(tri, g2) + _dot(tri, g3)
    btot = jnp.sum(g, axis=0, keepdims=True)
    st = st_ref[...]
    vb = v.astype(BF16)

    o = _nt_dot((q * jnp.exp(b)).astype(BF16), st.astype(BF16))

    rowv = lax.broadcasted_iota(jnp.int32, (T, 1), 0)
    sub_row = lax.broadcasted_iota(jnp.int32, (SUB, 1), 0)
    lane = lax.broadcasted_iota(jnp.int32, (1, T), 1)
    n_sub = T // SUB
    blocks = []
    for i in range(n_sub):
        lo, hi = SUB * i, SUB * (i + 1)
        has_off = (i < n_sub - 1) if reverse else (i > 0)
        if has_off:
            r = b[hi:hi + 1] if reverse else b[lo - 1:lo]
            qt = (q[lo:hi] * jnp.exp(b[lo:hi] - r)).astype(BF16)
            kt = k * jnp.exp(jnp.minimum(r - b, 0.0))
            kt = jnp.where((rowv >= hi) if reverse else (rowv < lo), kt, 0.0).astype(BF16)
            blocks.append(_nt_dot(qt, kt))
        else:
            blocks.append(jnp.zeros((SUB, T), F32))
    for s in range(SUB):
        for i in range(n_sub):
            lo, hi = SUB * i, SUB * (i + 1)
            rs = lo + s
            e = jnp.exp(jnp.minimum(b[lo:hi] - b[rs:rs + 1], 0.0))
            cv = jnp.sum(q[lo:hi] * e * k[rs:rs + 1], axis=-1, keepdims=True)
            cv = jnp.where((sub_row <= s) if reverse else (sub_row >= s), cv, 0.0)
            blocks[i] = jnp.where(lane == rs, cv, blocks[i])
    a = jnp.concatenate(blocks, axis=0)
    o_ref[...] = o + _dot(a.astype(BF16), vb)

    kd = (k * jnp.exp(btot - b)).astype(BF16)
    st_new = st * jnp.exp(btot) + _dot(v.T.astype(BF16), kd)
    st_ref[...] = st_new
    sf_ref[...] = st_new.T


def _hgrn_scan(z, lb16, s0, rows):
    R = z.shape[0]
    T = REC_TILE
    nt = R // T
    nseg = rows.nb_p + rows.nb_s

    def specs(d):
        tile = (lambda t: nt - 1 - t) if d else (lambda t: t)
        zblk = lambda off: pl.BlockSpec((T, LANES), lambda h, t: (tile(t), off // LANES + h))
        ins = [zblk(OFF_RQ), zblk(OFF_RI), zblk(OFF_FB if d else OFF_FF),
               pl.BlockSpec((1, 1, LANES), lambda h, t: (d * N_HEADS_R + h, 0, 0)),
               pl.BlockSpec((None, None, None, HEAD_DIM_R, HEAD_DIM_R),
                            lambda h, t: (d, rows.seg_index(tile(t), T), h, 0, 0))]
        outs = [pl.BlockSpec((T, LANES), lambda h, t: (tile(t), h)),
                pl.BlockSpec((None, None, HEAD_DIM_R, HEAD_DIM_R),
                             lambda h, t: (rows.seg_index(tile(t), T), h, 0, 0))]
        return ins, outs

    (in_f, out_f), (in_b, out_b) = specs(0), specs(1)
    shapes = [jax.ShapeDtypeStruct((R, REC_W), F32),
              jax.ShapeDtypeStruct((nseg, N_HEADS_R, HEAD_DIM_R, HEAD_DIM_R), F32)]
    o_f, s_f, o_b, s_b = pl.pallas_call(
        functools.partial(_hgrn_kernel, rows=rows, nt=nt),
        grid=(N_HEADS_R, nt),
        in_specs=in_f + in_b,
        out_specs=out_f + out_b,
        out_shape=shapes + shapes,
        scratch_shapes=[pltpu.VMEM((HEAD_DIM_R, HEAD_DIM_R), F32)] * 2,
        compiler_params=_cparams("parallel", "arbitrary"),
    )(z, z, z, lb16, s0, z, z, z, lb16, s0)
    return o_f, s_f, o_b, s_b


def _rec_post_kernel(of_ref, ob_ref, zg_ref, g_ref, o_ref):
    for h in range(N_HEADS_R):
        sl = slice(h * LANES, (h + 1) * LANES)
        o = of_ref[:, sl] + ob_ref[:, sl]
        y = o * lax.rsqrt(jnp.mean(o * o, axis=-1, keepdims=True) + EPS) * g_ref[...]
        o_ref[:, sl] = (y * _silu(zg_ref[:, sl])).astype(o_ref.dtype)


def _rec_post(o_f, o_b, z, g, rows):
    R = z.shape[0]
    tm = min(256, rows.seq_p)
    blk = pl.BlockSpec((tm, REC_W), lambda i: (i, 0))
    return pl.pallas_call(
        _rec_post_kernel,
        grid=(R // tm,),
        in_specs=[blk, blk, pl.BlockSpec((tm, REC_W), lambda i: (i, OFF_RG // REC_W)),
                  pl.BlockSpec((1, LANES), lambda i: (0, 0))],
        out_specs=blk,
        out_shape=jax.ShapeDtypeStruct((R, REC_W), BF16),
        compiler_params=_cparams("parallel"),
    )(o_f, o_b, z, g.reshape(1, LANES))


def _merge_kernel(uap_ref, uas_ref, uc_ref, uh_ref, wa_ref, wc_ref, wh_ref, ga_ref, gc_ref, gh_ref, o_ref, *, npt):
    ua = jnp.where(pl.program_id(1) < npt, uap_ref[...], uas_ref[...])
    m = jax.nn.sigmoid(ga_ref[...]) * _dot(ua, wa_ref[...])
    m = m + jax.nn.sigmoid(gc_ref[...]) * _dot(uc_ref[...], wc_ref[...])
    m = m + jax.nn.sigmoid(gh_ref[...]) * _dot(uh_ref[...], wh_ref[...])
    o_ref[...] = m.astype(o_ref.dtype)


def _merge(ua_p, ua_s, uc, uh, wa, wc, wh, z):
    R = uc.shape[0]
    D = wa.shape[1]
    tm, tn = min(512, ua_p.shape[0]), min(512, D)
    npt = ua_p.shape[0] // tm
    gb = OFF_GATE // tn
    nd = D // tn
    act = lambda w: pl.BlockSpec((tm, w), lambda j, i: (i, 0))
    wgt = lambda w: pl.BlockSpec((w, tn), lambda j, i: (0, j))
    gate = lambda br: pl.BlockSpec((tm, tn), lambda j, i: (i, gb + br * nd + j))
    return pl.pallas_call(
        functools.partial(_merge_kernel, npt=npt),
        grid=(D // tn, R // tm),
        in_specs=[pl.BlockSpec((tm, ATTN_W), lambda j, i: (jnp.minimum(i, npt - 1), 0)),
                  pl.BlockSpec((tm, ATTN_W), lambda j, i: (jnp.maximum(i - npt, 0), 0)),
                  act(CONV_W), act(REC_W), wgt(ATTN_W), wgt(CONV_W), wgt(REC_W),
                  gate(0), gate(1), gate(2)],
        out_specs=pl.BlockSpec((tm, tn), lambda j, i: (i, j)),
        out_shape=jax.ShapeDtypeStruct((R, D), BF16),
        compiler_params=_cparams("parallel", "parallel"),
    )(ua_p, ua_s, uc, uh, wa, wc, wh, z, z, z)


def _router_kernel(w_ref, h_ref, o_ref):
    s = _nt_dot(w_ref[...], h_ref[...].astype(BF16))
    e = jnp.exp(s - jnp.max(s, axis=0, keepdims=True))
    o_ref[...] = e / jnp.sum(e, axis=0, keepdims=True)


def _router(h, w_router):
    R, D = h.shape
    tm = min(512, R)
    wt = w_router.T.astype(BF16)
    return pl.pallas_call(
        _router_kernel,
        grid=(R // tm,),
        in_specs=[pl.BlockSpec((N_EXPERTS, D), lambda i: (0, 0)),
                  pl.BlockSpec((tm, D), lambda i: (i, 0))],
        out_specs=pl.BlockSpec((N_EXPERTS, tm), lambda i: (0, i)),
        out_shape=jax.ShapeDtypeStruct((N_EXPERTS, R), F32),
        compiler_params=_cparams("parallel"),
    )(wt, h)


TOK_RADIX = 64


def _topk_kernel(a_ref, slot_ref, idx_ref, gs_ref, lhs_ref, pm_ref, *, n, cap, capp, blocks_per_route):
    x = a_ref[...]
    bits = pltpu.bitcast(x, jnp.int32)

    def search(i, thr):
        cand = thr | (1 << (30 - i))
        cnt = jnp.sum(jnp.where(bits >= cand, 1.0, 0.0), axis=-1, keepdims=True)
        return jnp.where(cnt >= cap, cand, thr)

    thr = lax.fori_loop(0, 31, search, jnp.zeros((N_EXPERTS, 1), jnp.int32))
    gt = bits > thr
    eq = bits == thr
    need = cap - jnp.sum(jnp.where(gt, 1.0, 0.0), axis=-1, keepdims=True)

    cb = min(n, 512)
    ci = lax.broadcasted_iota(jnp.int32, (cb, cb), 0)
    cj = lax.broadcasted_iota(jnp.int32, (cb, cb), 1)
    upper = jnp.where(ci <= cj, 1.0, 0.0).astype(BF16)

    def prefix(mask01):
        outs, run = [], jnp.zeros((N_EXPERTS, 1), F32)
        for c in range(n // cb):
            inc = _dot(mask01[:, c * cb:(c + 1) * cb].astype(BF16), upper) + run
            outs.append(inc)
            run = inc[:, cb - 1:cb]
        return jnp.concatenate(outs, axis=1) if len(outs) > 1 else outs[0]

    eq01 = jnp.where(eq, 1.0, 0.0)
    sel = gt | (eq & (prefix(eq01) - eq01 < need))
    sel01 = jnp.where(sel, 1.0, 0.0)
    pos = prefix(sel01) - sel01
    blk = pl.program_id(0) % blocks_per_route
    slot_ref[...] = jnp.where(sel, pos.astype(jnp.int32) + blk * cap, -1)

    tok = lax.broadcasted_iota(jnp.int32, (1, n), 1)
    gsel = jnp.where(sel, x, 0.0)
    g1 = gsel.astype(BF16).astype(F32)
    r1 = gsel - g1
    g2 = r1.astype(BF16).astype(F32)
    g3 = (r1 - g2).astype(BF16).astype(F32)
    pm_ref[...] = jnp.where(sel, pos, -1.0)
    lhs_ref[...] = jnp.zeros(lhs_ref.shape, F32)
    for e in range(N_EXPERTS):
        lhs_ref[e, 0:1, :] = (tok // TOK_RADIX).astype(F32)
        lhs_ref[e, 1:2, :] = (tok % TOK_RADIX).astype(F32)
        lhs_ref[e, 2:3, :] = g1[e:e + 1]
        lhs_ref[e, 3:4, :] = g2[e:e + 1]
        lhs_ref[e, 4:5, :] = g3[e:e + 1]
    p_iota = lax.broadcasted_iota(jnp.int32, (capp, 1), 0).astype(F32)

    def per_expert(e, carry):
        def per_chunk(c, res):
            cs = pl.ds(pl.multiple_of(c * cb, cb), cb)
            onehot = jnp.where(pm_ref[pl.ds(e, 1), cs] == p_iota, 1.0, 0.0).astype(BF16)
            return res + _nt_dot(lhs_ref[e, :, cs].astype(BF16), onehot)
        res = lax.fori_loop(0, n // cb, per_chunk, jnp.zeros((8, capp), F32))
        idx = (res[0:1] * TOK_RADIX + res[1:2]).astype(jnp.int32) + blk * n
        idx_ref[pl.ds(e, 1)] = idx.reshape(1, 1, capp)
        gs_ref[pl.ds(e, 1)] = (res[2:3] + res[3:4] + res[4:5]).reshape(1, 1, capp)
        return carry

    lax.fori_loop(0, N_EXPERTS, per_expert, 0)


def _topk(aff_t, n, cap, col0, ncols, blocks_per_route):
    nb = ncols // n
    off = col0 // n
    capp = -(-cap // LANES) * LANES
    blk_in = pl.BlockSpec((N_EXPERTS, n), lambda s: (0, off + s))
    blk_slot = pl.BlockSpec((None, N_EXPERTS, 1, capp), lambda s: (s, 0, 0, 0))
    return pl.pallas_call(
        functools.partial(_topk_kernel, n=n, cap=cap, capp=capp, blocks_per_route=blocks_per_route),
        grid=(nb,),
        in_specs=[blk_in],
        out_specs=[pl.BlockSpec((N_EXPERTS, n), lambda s: (0, s)), blk_slot, blk_slot],
        out_shape=[jax.ShapeDtypeStruct((N_EXPERTS, ncols), jnp.int32),
                   jax.ShapeDtypeStruct((nb, N_EXPERTS, 1, capp), jnp.int32),
                   jax.ShapeDtypeStruct((nb, N_EXPERTS, 1, capp), F32)],
        scratch_shapes=[pltpu.VMEM((N_EXPERTS, 8, n), F32), pltpu.VMEM((N_EXPERTS, n), F32)],
        compiler_params=_cparams("parallel"),
    )(aff_t)


def _gather_kernel(idx_ref, h_hbm, x_ref, buf_ref, sem, *, cap):
    s, e = pl.program_id(0), pl.program_id(1)
    r = s * N_EXPERTS + e

    def row_copy(p, src_row):
        return pltpu.make_async_copy(h_hbm.at[pl.ds(src_row, 1)], buf_ref.at[pl.ds(p, 1)], sem)

    def start(p, carry):
        row_copy(p, idx_ref[r, p]).start()
        return carry

    def wait(p, carry):
        row_copy(p, 0).wait()
        return carry

    lax.fori_loop(0, cap, start, 0, unroll=8)
    lax.fori_loop(0, cap, wait, 0, unroll=8)
    x_ref[0] = buf_ref[...].astype(x_ref.dtype)


def _gather(idx, h, nseg, cap):
    R, D = h.shape
    return pl.pallas_call(
        functools.partial(_gather_kernel, cap=cap),
        grid_spec=pltpu.PrefetchScalarGridSpec(
            num_scalar_prefetch=1,
            grid=(nseg, N_EXPERTS),
            in_specs=[pl.BlockSpec(memory_space=pl.ANY)],
            out_specs=pl.BlockSpec((1, cap, D), lambda s, e, idx: (e, s, 0)),
            scratch_shapes=[pltpu.VMEM((cap, D), F32), pltpu.SemaphoreType.DMA(())]),
        out_shape=jax.ShapeDtypeStruct((N_EXPERTS, nseg * cap, D), BF16),
        compiler_params=_cparams("arbitrary", "arbitrary"),
    )(idx, h)


def _ffn_up_kernel(x_ref, wg_ref, wu_ref, o_ref):
    x = x_ref[0]
    a = _dot(x, wg_ref[...].astype(BF16))
    b = _dot(x, wu_ref[...].astype(BF16))
    o_ref[0] = (_silu(a) * b).astype(o_ref.dtype)


def _ffn_up(x, wg, wu, l):
    E, M, D = x.shape
    Fh = wg.shape[3]
    tm = M // 2 if (M // 2) % 16 == 0 else M
    tf = min(256, Fh)
    wblk = pl.BlockSpec((None, None, D, tf), lambda e, i, j: (l, e, 0, j))
    return pl.pallas_call(
        _ffn_up_kernel,
        grid=(E, M // tm, Fh // tf),
        in_specs=[pl.BlockSpec((1, tm, D), lambda e, i, j: (e, i, 0)), wblk, wblk],
        out_specs=pl.BlockSpec((1, tm, tf), lambda e, i, j: (e, i, j)),
        out_shape=jax.ShapeDtypeStruct((E, M, Fh), BF16),
        compiler_params=_cparams("parallel", "parallel", "parallel"),
    )(x, wg, wu)


def _ffn_down_kernel(h_ref, w_ref, g_ref, o_ref):
    o_ref[0] = (_dot(h_ref[0], w_ref[...].astype(BF16)) * g_ref[0]).astype(o_ref.dtype)


def _ffn_down(hid, wd, gate_col, l):
    E, M, Fh = hid.shape
    D = wd.shape[3]
    tm = M // 2 if (M // 2) % 16 == 0 else M
    tn = min(1024, D)
    return pl.pallas_call(
        _ffn_down_kernel,
        grid=(E, M // tm, D // tn),
        in_specs=[pl.BlockSpec((1, tm, Fh), lambda e, i, j: (e, i, 0)),
                  pl.BlockSpec((None, None, Fh, tn), lambda e, i, j: (l, e, 0, j)),
                  pl.BlockSpec((1, tm, 1), lambda e, i, j: (e, i, 0))],
        out_specs=pl.BlockSpec((1, tm, tn), lambda e, i, j: (e, i, j)),
        out_shape=jax.ShapeDtypeStruct((E, M, D), BF16),
        compiler_params=_cparams("parallel", "parallel", "parallel"),
    )(hid, wd, gate_col)


COMBINE_TOKENS = 256
COMBINE_WINDOW = 64
ROW_ALIGN = 16


def _combine_kernel(win_ref, nr_ref, row_ref, y_hbm, x_ref, g_ref, o_ref, buf_ref, sem, *, m_rows):
    i = pl.program_id(0)
    W = COMBINE_WINDOW
    rows_te = row_ref[...]
    lane = lax.broadcasted_iota(jnp.int32, (1, LANES), 1)

    def window_copy(e, start):
        return pltpu.make_async_copy(y_hbm.at[e, pl.ds(start, W)], buf_ref.at[pl.ds(e * W, W)], sem)

    def one_round(r, acc):
        bounds = []
        for e in range(N_EXPERTS):
            lo = win_ref[i, e] + r * W
            start = jnp.minimum(lo, m_rows - W)
            window_copy(e, pl.multiple_of(start, ROW_ALIGN)).start()
            bounds.append((lo, start))
        for e in range(N_EXPERTS):
            window_copy(e, 0).wait()
        groups = []
        for g in range(N_EXPERTS // 2):
            rel = []
            for e in (2 * g, 2 * g + 1):
                lo, start = bounds[e]
                col = rows_te[:, e:e + 1]
                rel.append(jnp.where(col >= lo, col - start, -1))
            target = jnp.where(lane < W, rel[0], rel[1] + W)
            groups.append(jnp.where(target == lane, 1.0, 0.0).astype(BF16))
        onehot = jnp.concatenate(groups, axis=1)
        return acc + _dot(onehot, buf_ref[...])

    acc = lax.fori_loop(0, nr_ref[i], one_round, jnp.zeros(x_ref.shape, F32))
    o_ref[...] = x_ref[...] + g_ref[0] * acc


def _combine(slot, y, x, mod, rows, i_gate, n, cap):
    R, D = x.shape
    E, M = y.shape[0], y.shape[1]
    T, W = min(COMBINE_TOKENS, n), COMBINE_WINDOW
    nt, tps, nseg = R // T, n // T, R // n
    sel = slot >= 0
    row = jnp.where(sel, slot + (jnp.arange(R) // n * cap)[None, :], -1)
    cnt = jnp.sum(sel.reshape(E, nseg, tps, T), axis=-1)
    lo = jnp.cumsum(cnt, axis=-1) - cnt + (jnp.arange(nseg) * cap)[None, :, None]
    lo_al = lo // ROW_ALIGN * ROW_ALIGN
    rounds = jnp.where(cnt > 0, (lo + cnt - lo_al + W - 1) // W, 0)
    nr = jnp.max(rounds, axis=0).reshape(nt).astype(jnp.int32)
    win = lo_al.reshape(E, nt).T.astype(jnp.int32)
    tok = lambda w: pl.BlockSpec((T, w), lambda i, *_: (i, 0))
    return pl.pallas_call(
        functools.partial(_combine_kernel, m_rows=M),
        grid_spec=pltpu.PrefetchScalarGridSpec(
            num_scalar_prefetch=2,
            grid=(nt,),
            in_specs=[tok(E), pl.BlockSpec(memory_space=pl.ANY), tok(D),
                      pl.BlockSpec((1, 1, D), lambda i, *_: (rows.mod_row(i, T) * N_MOD + i_gate, 0, 0))],
            out_specs=tok(D),
            scratch_shapes=[pltpu.VMEM((E * W, D), BF16), pltpu.SemaphoreType.DMA(())]),
        out_shape=jax.ShapeDtypeStruct((R, D), F32),
        compiler_params=_cparams("arbitrary"),
    )(win, nr, row.T.astype(jnp.int32), y, x, mod)


def _moe(h2, x, mod, rows, l, w_router, w_gate, w_up, w_down):
    P, S, E = rows.P, rows.seq_s, N_EXPERTS
    nseg = rows.R // S
    aff_t = _router(h2, w_router[l])
    cap_p = CAP_FACTOR * rows.seq_p // E
    cap_s = CAP_FACTOR * S // E
    slot_p, idx_p, gs_p = _topk(aff_t, rows.seq_p, cap_p, 0, P, rows.nb_p)
    slot_s, idx_s, gs_s = _topk(aff_t, S, cap_s, P, rows.R - P, 1)

    def per_block(a_p, a_s):
        a_p = a_p[:, :, 0, :cap_p].transpose(1, 0, 2).reshape(1, E, rows.nb_p * cap_p)
        return jnp.concatenate([a_p, a_s[:, :, 0, :cap_s]], axis=0)

    idx = per_block(idx_p, idx_s) + (jnp.arange(nseg, dtype=jnp.int32) * S)[:, None, None]
    gate_col = per_block(gs_p, gs_s).transpose(1, 0, 2).reshape(E, nseg * cap_s, 1)
    xs = _gather(idx.reshape(nseg * E, cap_s), h2, nseg, cap_s)
    ys = _ffn_down(_ffn_up(xs, w_gate, w_up, l), w_down, gate_col, l)
    slot = jnp.concatenate([slot_p, slot_s], axis=1)
    return _combine(slot, ys, x, mod, rows, 5, S, cap_s)


def _layer(x, l, rows, mod, p, tables, lam_val, lam_init, lbs, cache_k, cache_v, state_hgrn):
    R, D = x.shape
    P = rows.P
    h = _modulate(x, p["norm1_g"][l], mod, rows, 0, 1, BF16)
    z = _linear(h, p["w_in"], l, 1024, 512, F32)

    gq, gk = p["qk_norm_g"][l, 0], p["qk_norm_g"][l, 1]
    tile2 = lambda g: jnp.tile(g, 2).reshape(1, LANES)
    offset = 1.02 * HEAD_DIM_A * Q_SCALE * jnp.max(jnp.abs(gq)) * jnp.max(jnp.abs(gk))
    pad = jnp.zeros((8, LANES), F32).at[0, HEAD_DIM_A].set(-offset).at[1, HEAD_DIM_A].set(1.0)
    qa, qb, ka, kb, k_f, vx = _qkv_prep(z, tile2(gq), tile2(gk), pad, tables[0], tables[1], rows)
    prm = jnp.zeros((8, LANES), F32)
    prm = prm.at[0].set(p["attn_norm_g"][l]).at[1].set(lam_val).at[2].set(1.0 - lam_init)
    fast_ok = offset <= FAST_SOFTMAX_MAX_OFFSET
    ck = cache_k[:, l]
    kpad = jnp.broadcast_to(pad[1, HEAD_DIM_A:], ck.shape[:3] + (HEAD_DIM_A,))
    ctx_k = [jnp.concatenate([ck[:, :, :, m], kpad], axis=-1).reshape(rows.nb_s, -1, ATTN_W).astype(BF16)
             for m in range(2)]
    cv = cache_v[:, l]
    ctx_v = jnp.concatenate([cv, jnp.ones_like(cv)], axis=-1).reshape(rows.nb_s, -1, 2 * ATTN_W).astype(BF16)
    ua_p = _attention(prm, qa, qb, ka, kb, vx, None, 0, rows.nb_p, rows.seq_p, 256, 256, fast_ok)
    ua_s = _attention(prm, qa, qb, ka, kb, vx, (ctx_k[0], ctx_k[1], ctx_v), P, rows.nb_s, rows.seq_s,
                      2048, 512, fast_ok)

    uc = _conv_branch(z, p["conv_w"][l], p["conv_b"][l], p["conv_ln_g"][l], p["conv_ln_b"][l], rows)

    s0 = jnp.concatenate([jnp.zeros((2, rows.nb_p, N_HEADS_R, HEAD_DIM_R, HEAD_DIM_R), F32),
                          jnp.swapaxes(state_hgrn[:, l], 0, 1).astype(F32)], axis=1)
    lb16 = lbs[l].reshape(2 * N_HEADS_R, 1, HEAD_DIM_R)
    o_f, s_f, o_b, s_b = _hgrn_scan(z, lb16, s0, rows)
    uh = _rec_post(o_f, o_b, z, p["hgrn_norm_g"][l], rows)

    bf = lambda w: w[l].astype(BF16)
    m = _merge(ua_p, ua_s, uc, uh, bf(p["w_attn_o"]), bf(p["w_conv_o"]), bf(p["w_hgrn_o"]), z)
    x = _linear_res(m, bf(p["w_out"]), x, mod, rows, 2, 512, 1024)

    h2 = _modulate(x, p["norm2_g"][l], mod, rows, 3, 4, F32)
    x = _moe(h2, x, mod, rows, l, p["w_router"], p["w_e_gate"], p["w_e_up"], p["w_e_down"])

    new_k = k_f[:P].reshape(rows.nb_p, rows.seq_p, N_HEADS_A, 2, HEAD_DIM_A)
    new_v = z[:P, OFF_V:OFF_V + ATTN_W].reshape(rows.nb_p, rows.seq_p, N_HEADS_A, 2 * HEAD_DIM_A)
    new_s = jnp.stack([s_f[:rows.nb_p], s_b[:rows.nb_p]], axis=1)
    return x, new_k, new_v, new_s


def _rope_tables(n_tok, tm):
    t = jnp.arange(n_tok)
    row = (t // GRID_W).astype(F32)
    col = (t % GRID_W).astype(F32)
    inv = ROPE_BASE ** (-jnp.arange(0, AXIS_DIM, 2, dtype=F32) / AXIS_DIM)
    ar, ac = row[:, None] * inv, col[:, None] * inv
    ang = jnp.concatenate([ar, ar, ac, ac], axis=-1)
    sign = jnp.where((jnp.arange(HEAD_DIM_A) & (AXIS_DIM // 2)) != 0, 1.0, -1.0).astype(F32)
    cos = jnp.tile(jnp.cos(ang), (1, 2))
    sin = jnp.tile(jnp.sin(ang) * sign, (1, 2))
    cos = jnp.concatenate([jnp.ones((tm, LANES), F32), cos], axis=0)
    sin = jnp.concatenate([jnp.zeros((tm, LANES), F32), sin], axis=0)
    return cos, sin


def kernel(x_prompt, x_sample, cache_k, cache_v, state_hgrn, c, c_ctx, norm1_g, norm2_g, w_ada, b_ada, w_in, qk_norm_g, lam, attn_norm_g, w_attn_o, conv_w, conv_b, conv_ln_g, conv_ln_b, w_conv_o, hgrn_lb, hgrn_norm_g, w_hgrn_o, w_out, w_router, w_e_gate, w_e_up, w_e_down):
    nb_p, seq_p, D = x_prompt.shape
    nb_s, seq_s, _ = x_sample.shape
    depth = w_in.shape[0]
    rows = _Rows(nb_p, seq_p, nb_s, seq_s)
    p = dict(norm1_g=norm1_g, norm2_g=norm2_g, w_in=w_in, qk_norm_g=qk_norm_g, attn_norm_g=attn_norm_g,
             w_attn_o=w_attn_o, conv_w=conv_w, conv_b=conv_b, conv_ln_g=conv_ln_g, conv_ln_b=conv_ln_b,
             w_conv_o=w_conv_o, hgrn_norm_g=hgrn_norm_g, w_hgrn_o=w_hgrn_o, w_out=w_out, w_router=w_router,
             w_e_gate=w_e_gate, w_e_up=w_e_up, w_e_down=w_e_down)

    sm = jax.nn.softmax(hgrn_lb.astype(F32), axis=0)
    lbs = jnp.cumsum(sm, axis=0) - sm[0]
    tables = _rope_tables(seq_s, min(256, seq_p))
    cond8 = jnp.concatenate([c_ctx[None], c, jnp.zeros((8 - 1 - nb_s, D), F32)], axis=0)

    x = jnp.concatenate([x_prompt.reshape(rows.P, D), x_sample.reshape(nb_s * seq_s, D)], axis=0)
    new_k, new_v, new_s = [], [], []
    for l in range(depth):
        lam_init = 0.8 - 0.6 * math.exp(-0.3 * l)
        lam_l = lam[l].astype(F32)
        lam_val = jnp.exp(jnp.sum(lam_l[0] * lam_l[1])) - jnp.exp(jnp.sum(lam_l[2] * lam_l[3])) + lam_init
        mod = _ada(cond8, w_ada, b_ada, l).reshape(8 * N_MOD, 1, D)
        x, k_l, v_l, s_l = _layer(x, l, rows, mod, p, tables, lam_val, lam_init, lbs, cache_k, cache_v, state_hgrn)
        new_k.append(k_l)
        new_v.append(v_l)
        new_s.append(s_l)
    y_prompt = x[:rows.P].reshape(nb_p, seq_p, D)
    y_sample = x[rows.P:].reshape(nb_s, seq_s, D)
    return (y_prompt, y_sample, jnp.stack(new_k, axis=1), jnp.stack(new_v, axis=1), jnp.stack(new_s, axis=1))
```

```python
import functools
import math

import jax
import jax.numpy as jnp
from jax import lax
from jax.experimental import pallas as pl
from jax.experimental.pallas import tpu as pltpu

F32 = jnp.float32
BF16 = jnp.bfloat16

GRID_W = 64
N_HEADS_A = 16
HEAD_DIM_A = 64
ATTN_W = N_HEADS_A * 2 * HEAD_DIM_A
AXIS_DIM = HEAD_DIM_A // 2
ROPE_BASE = 10000.0
CONV_W = 1024
CONV_K = 31
N_HEADS_R = 8
HEAD_DIM_R = 128
REC_W = N_HEADS_R * HEAD_DIM_R
N_EXPERTS = 16
CAP_FACTOR = 2
N_BRANCH = 3
N_MOD = 6
EPS = 1e-6

LANES = 128
REC_TILE = 128
REC_SUB = 16
CONV_HALO = 16
VMEM_LIMIT = 56 * 1024 * 1024

OFF_Q = 0
OFF_K = ATTN_W
OFF_V = 2 * ATTN_W
OFF_GLU = 3 * ATTN_W
OFF_RQ = OFF_GLU + 2 * CONV_W
OFF_RI = OFF_RQ + REC_W
OFF_FF = OFF_RI + REC_W
OFF_FB = OFF_FF + REC_W
OFF_RG = OFF_FB + REC_W
OFF_GATE = OFF_RG + REC_W


def _cparams(*sem):
    return pltpu.CompilerParams(dimension_semantics=sem, vmem_limit_bytes=VMEM_LIMIT)


def _nt_dot(a, b):
    return lax.dot_general(a, b, (((1,), (1,)), ((), ())), preferred_element_type=F32)


def _dot(a, b):
    return jnp.dot(a, b, preferred_element_type=F32)


def _silu(x):
    return x * jax.nn.sigmoid(x)


class _Rows:
    def __init__(self, nb_p, seq_p, nb_s, seq_s):
        self.nb_p, self.seq_p, self.nb_s, self.seq_s = nb_p, seq_p, nb_s, seq_s
        self.P = nb_p * seq_p
        self.R = self.P + nb_s * seq_s
        assert self.P == seq_s, "prompt rows are routed as one sample-sized block"

    def mod_row(self, i, tm):
        r = i * tm
        return jnp.where(r < self.P, 0, 1 + (r - self.P) // self.seq_s)

    def seg_local(self, i, tm):
        tp, ts, npt = self.seq_p // tm, self.seq_s // tm, self.P // tm
        in_p = i < npt
        local = jnp.where(in_p, i % tp, (i - npt) % ts)
        per = jnp.where(in_p, tp, ts)
        return local, per

    def seg_index(self, i, tm):
        tp, ts, npt = self.seq_p // tm, self.seq_s // tm, self.P // tm
        return jnp.where(i < npt, i // tp, self.nb_p + (i - npt) // ts)


def _ada_kernel(c_ref, w_ref, b_ref, o_ref):
    a = _silu(c_ref[...]).astype(BF16)
    o_ref[...] = _dot(a, w_ref[...].astype(BF16)) + b_ref[...]


def _ada(cond8, w, b, l):
    _, D, N = w.shape
    tn = min(N, 512)
    return pl.pallas_call(
        _ada_kernel,
        grid=(N // tn,),
        in_specs=[pl.BlockSpec((8, D), lambda j: (0, 0)),
                  pl.BlockSpec((None, D, tn), lambda j: (l, 0, j)),
                  pl.BlockSpec((None, 1, tn), lambda j: (l, 0, j))],
        out_specs=pl.BlockSpec((8, tn), lambda j: (0, j)),
        out_shape=jax.ShapeDtypeStruct((8, N), F32),
        compiler_params=_cparams("parallel"),
    )(cond8, w, b.reshape(b.shape[0], 1, N))


def _modulate_kernel(x_ref, g_ref, sh_ref, sc_ref, o_ref):
    x = x_ref[...]
    y = x * lax.rsqrt(jnp.mean(x * x, axis=-1, keepdims=True) + EPS) * g_ref[...]
    o_ref[...] = (y * (1.0 + sc_ref[0]) + sh_ref[0]).astype(o_ref.dtype)


def _modulate(x, g, mod, rows, i_shift, i_scale, out_dtype):
    R, D = x.shape
    tm = min(256, rows.seq_p)
    return pl.pallas_call(
        _modulate_kernel,
        grid=(R // tm,),
        in_specs=[pl.BlockSpec((tm, D), lambda i: (i, 0)),
                  pl.BlockSpec((1, D), lambda i: (0, 0)),
                  pl.BlockSpec((1, 1, D), lambda i: (rows.mod_row(i, tm) * N_MOD + i_shift, 0, 0)),
                  pl.BlockSpec((1, 1, D), lambda i: (rows.mod_row(i, tm) * N_MOD + i_scale, 0, 0))],
        out_specs=pl.BlockSpec((tm, D), lambda i: (i, 0)),
        out_shape=jax.ShapeDtypeStruct((R, D), out_dtype),
        compiler_params=_cparams("parallel"),
    )(x, g.reshape(1, D), mod, mod)


def _linear_kernel(a_ref, w_ref, o_ref):
    o_ref[...] = _dot(a_ref[...], w_ref[...].astype(BF16)).astype(o_ref.dtype)


def _linear(a, w, l, tm, tn, out_dtype):
    M, K = a.shape
    N = w.shape[2]
    tm, tn = min(tm, M), min(tn, N)
    return pl.pallas_call(
        _linear_kernel,
        grid=(M // tm, N // tn),
        in_specs=[pl.BlockSpec((tm, K), lambda i, j: (i, 0)),
                  pl.BlockSpec((None, K, tn), lambda i, j: (l, 0, j))],
        out_specs=pl.BlockSpec((tm, tn), lambda i, j: (i, j)),
        out_shape=jax.ShapeDtypeStruct((M, N), out_dtype),
        compiler_params=_cparams("parallel", "parallel"),
    )(a, w)


def _linear_res_kernel(a_ref, w_ref, x_ref, g_ref, o_ref):
    o_ref[...] = x_ref[...] + g_ref[0] * _dot(a_ref[...], w_ref[...])


def _linear_res(a, w, x, mod, rows, i_gate, tm, tn):
    M, K = a.shape
    N = w.shape[1]
    tm, tn = min(tm, rows.P), min(tn, N)
    return pl.pallas_call(
        _linear_res_kernel,
        grid=(N // tn, M // tm),
        in_specs=[pl.BlockSpec((tm, K), lambda j, i: (i, 0)),
                  pl.BlockSpec((K, tn), lambda j, i: (0, j)),
                  pl.BlockSpec((tm, tn), lambda j, i: (i, j)),
                  pl.BlockSpec((1, 1, tn), lambda j, i: (rows.mod_row(i, tm) * N_MOD + i_gate, 0, j))],
        out_specs=pl.BlockSpec((tm, tn), lambda j, i: (i, j)),
        out_shape=jax.ShapeDtypeStruct((M, N), F32),
        compiler_params=_cparams("parallel", "parallel"),
    )(a, w, x, mod)


LOG2E = math.log2(math.e)
Q_SCALE = HEAD_DIM_A ** -0.5 * LOG2E
FAST_SOFTMAX_MAX_OFFSET = 55.0


def _qkv_kernel(zq_ref, zk_ref, zv_ref, gq_ref, gk_ref, pad_ref, cos_ref, sin_ref,
                qa_ref, qb_ref, ka_ref, kb_ref, kf_ref, vx_ref):
    cos, sin = cos_ref[...], sin_ref[...]
    lane = lax.broadcasted_iota(jnp.int32, (1, LANES), 1)
    upper = (lane & (AXIS_DIM // 2)) != 0
    lower_map = lane < HEAD_DIM_A
    gi = lax.broadcasted_iota(jnp.int32, (LANES, LANES), 0) // HEAD_DIM_A
    gj = lax.broadcasted_iota(jnp.int32, (LANES, LANES), 1) // HEAD_DIM_A
    group = jnp.where(gi == gj, 1.0, 0.0).astype(BF16)
    qpad, kpad = pad_ref[0:1, :], pad_ref[1:2, :]
    ones = jnp.ones((zq_ref.shape[0], LANES), BF16)

    def norm_rope(x, g):
        x2 = x * x
        hi = x2.astype(BF16)
        lo = (x2 - hi.astype(F32)).astype(BF16)
        ss = _dot(hi, group) + _dot(lo, group)
        y = x * lax.rsqrt(ss * (1.0 / HEAD_DIM_A) + EPS) * g
        half = AXIS_DIM // 2
        rot = jnp.where(upper, pltpu.roll(y, half, 1), pltpu.roll(y, LANES - half, 1))
        return y * cos + rot * sin

    def split_maps(y, pad, a_ref, b_ref, sl):
        a_ref[:, sl] = jnp.where(lower_map, y, pad).astype(BF16)
        b_ref[:, sl] = jnp.where(lower_map, pltpu.roll(y, HEAD_DIM_A, 1), pad).astype(BF16)

    for h in range(N_HEADS_A):
        sl = slice(h * LANES, (h + 1) * LANES)
        q = norm_rope(zq_ref[:, sl], gq_ref[...]) * Q_SCALE
        split_maps(q, qpad, qa_ref, qb_ref, sl)
        k = norm_rope(zk_ref[:, sl], gk_ref[...])
        kf_ref[:, sl] = k
        split_maps(k, kpad, ka_ref, kb_ref, sl)
        vx_ref[:, 2 * h * LANES:(2 * h + 1) * LANES] = zv_ref[:, sl].astype(BF16)
        vx_ref[:, (2 * h + 1) * LANES:(2 * h + 2) * LANES] = ones


def _qkv_prep(z, gq, gk, pad, cos_t, sin_t, rows):
    R = z.shape[0]
    tm = min(256, rows.seq_p)
    npt, ts = rows.P // tm, rows.seq_s // tm
    tab = lambda i: (jnp.where(i < npt, 0, 1 + (i - npt) % ts), 0)
    blk = lambda c: pl.BlockSpec((tm, ATTN_W), lambda i: (i, c))
    bf = jax.ShapeDtypeStruct((R, ATTN_W), BF16)
    return pl.pallas_call(
        _qkv_kernel,
        grid=(R // tm,),
        in_specs=[blk(0), blk(1), blk(2),
                  pl.BlockSpec((1, LANES), lambda i: (0, 0)),
                  pl.BlockSpec((1, LANES), lambda i: (0, 0)),
                  pl.BlockSpec((8, LANES), lambda i: (0, 0)),
                  pl.BlockSpec((tm, LANES), tab),
                  pl.BlockSpec((tm, LANES), tab)],
        out_specs=[blk(0)] * 5 + [pl.BlockSpec((tm, 2 * ATTN_W), lambda i: (i, 0))],
        out_shape=[bf, bf, bf, bf, jax.ShapeDtypeStruct((R, ATTN_W), F32),
                   jax.ShapeDtypeStruct((R, 2 * ATTN_W), BF16)],
        compiler_params=_cparams("parallel"),
    )(z, z, z, gq, gk, pad, cos_t, sin_t)


def _attn_kernel(prm_ref, qa_ref, qb_ref, ka_ref, kb_ref, vx_ref, *rest, tk, n_ctx, n_self, online):
    if n_ctx:
        kac_ref, kbc_ref, vxc_ref, o_ref, acc_ref, m_ref = rest
    else:
        o_ref, acc_ref, m_ref = rest
    acc_ref[...] = jnp.zeros(acc_ref.shape, F32)
    if online:
        m_ref[...] = jnp.full(m_ref.shape, -jnp.inf, F32)
    qs = (qa_ref[...], qb_ref[...])

    def chunk(k_refs, v_ref, j):
        ks = pl.ds(pl.multiple_of(j * tk, tk), tk)
        v = v_ref[ks, :]
        for mi in range(2):
            s = _nt_dot(qs[mi], k_refs[mi][ks, :])
            if online:
                m_old = m_ref[mi]
                m_new = jnp.maximum(m_old, jnp.max(s, axis=-1, keepdims=True))
                p = jnp.exp2(s - m_new).astype(BF16)
                acc_ref[mi] = jnp.exp2(m_old - m_new) * acc_ref[mi] + _dot(p, v)
                m_ref[mi] = m_new
            else:
                acc_ref[mi] += _dot(jnp.exp2(s).astype(BF16), v)

    def loop(n, k_refs, v_ref):
        def body(j, carry):
            chunk(k_refs, v_ref, j)
            return carry
        lax.fori_loop(0, n, body, 0)

    if n_ctx:
        loop(n_ctx, (kac_ref, kbc_ref), vxc_ref)
    loop(n_self, (ka_ref, kb_ref), vx_ref)

    g, lam, post = prm_ref[0:1, :], prm_ref[1:2, :], prm_ref[2:3, :]
    a0, a1 = acc_ref[0], acc_ref[1]
    o = a0[:, :LANES] / a0[:, LANES:] - lam * (a1[:, :LANES] / a1[:, LANES:])
    y = o * lax.rsqrt(jnp.mean(o * o, axis=-1, keepdims=True) + EPS) * g
    o_ref[...] = (y * post).astype(o_ref.dtype)


def _attention(prm, qa, qb, ka, kb, vx, ctx, row0, nb, L, tq, tk, fast_ok):
    tq, tk = min(tq, L), min(tk, L)
    qt = L // tq
    qspec = pl.BlockSpec((tq, LANES), lambda b, h, i: (row0 // tq + b * qt + i, h))
    kspec = pl.BlockSpec((L, LANES), lambda b, h, i: (row0 // L + b, h))
    vspec = pl.BlockSpec((L, 2 * LANES), lambda b, h, i: (row0 // L + b, h))
    in_specs = [pl.BlockSpec((8, LANES), lambda b, h, i: (0, 0)), qspec, qspec, kspec, kspec, vspec]
    operands = [prm, qa, qb, ka, kb, vx]
    n_ctx = 0
    if ctx is not None:
        lc = ctx[0].shape[1]
        n_ctx = lc // tk
        in_specs += [pl.BlockSpec((None, lc, LANES), lambda b, h, i: (b, 0, h))] * 2
        in_specs += [pl.BlockSpec((None, lc, 2 * LANES), lambda b, h, i: (b, 0, h))]
        operands += list(ctx)

    def call(online, *ops):
        return pl.pallas_call(
            functools.partial(_attn_kernel, tk=tk, n_ctx=n_ctx, n_self=L // tk, online=online),
            grid=(nb, N_HEADS_A, qt),
            in_specs=in_specs,
            out_specs=pl.BlockSpec((tq, LANES), lambda b, h, i: (b * qt + i, h)),
            out_shape=jax.ShapeDtypeStruct((nb * L, ATTN_W), BF16),
            scratch_shapes=[pltpu.VMEM((2, tq, 2 * LANES), F32), pltpu.VMEM((2, tq, 1), F32)],
            compiler_params=_cparams("parallel", "parallel", "parallel"),
        )(*ops)

    return lax.cond(fast_ok, functools.partial(call, False), functools.partial(call, True), *operands)


def _conv_kernel(zc_ref, zp_ref, zn_ref, cw_ref, cb_ref, lng_ref, lnb_ref, o_ref, u_ref, c_ref, *, rows, tm):
    i = pl.program_id(0)
    local, per = rows.seg_local(i, tm)
    H = CONV_HALO

    def glu(z):
        return z[:, :CONV_W] * jax.nn.sigmoid(z[:, CONV_W:])

    u_ref[H:H + tm, :] = glu(zc_ref[...])
    u_ref[0:H, :] = jnp.where(local > 0, glu(zp_ref[...]), 0.0)
    u_ref[H + tm:2 * H + tm, :] = jnp.where(local < per - 1, glu(zn_ref[...]), 0.0)

    rb = min(tm, 128)

    def strip(s, carry):
        ls = pl.ds(pl.multiple_of(s * LANES, LANES), LANES)
        for r0 in range(0, tm, rb):
            acc = jnp.zeros((rb, LANES), F32)
            for j in range(CONV_K):
                start = r0 + j + H - CONV_K // 2
                acc = acc + u_ref[start:start + rb, ls] * cw_ref[j:j + 1, ls]
            c_ref[r0:r0 + rb, ls] = acc + cb_ref[:, ls]
        return carry

    lax.fori_loop(0, CONV_W // LANES, strip, 0)
    u = c_ref[...]
    mu = jnp.mean(u, axis=-1, keepdims=True)
    d = u - mu
    var = jnp.mean(d * d, axis=-1, keepdims=True)
    y = d * lax.rsqrt(var + EPS) * lng_ref[...] + lnb_ref[...]
    o_ref[...] = _silu(y).astype(o_ref.dtype)


def _conv_branch(z, cw, cb, lng, lnb, rows):
    R = z.shape[0]
    tm = min(256, rows.seq_p)
    H = CONV_HALO
    hb = tm // H
    cblk = OFF_GLU // (2 * CONV_W)
    cwp = jnp.concatenate([cw, jnp.zeros((8 - CONV_K % 8, CONV_W), F32)], axis=0)
    vec = pl.BlockSpec((1, CONV_W), lambda i: (0, 0))
    return pl.pallas_call(
        functools.partial(_conv_kernel, rows=rows, tm=tm),
        grid=(R // tm,),
        in_specs=[pl.BlockSpec((tm, 2 * CONV_W), lambda i: (i, cblk)),
                  pl.BlockSpec((H, 2 * CONV_W), lambda i: (jnp.maximum(i * hb - 1, 0), cblk)),
                  pl.BlockSpec((H, 2 * CONV_W), lambda i: (jnp.minimum((i + 1) * hb, R // H - 1), cblk)),
                  pl.BlockSpec(cwp.shape, lambda i: (0, 0)), vec, vec, vec],
        out_specs=pl.BlockSpec((tm, CONV_W), lambda i: (i, 0)),
        out_shape=jax.ShapeDtypeStruct((R, CONV_W), BF16),
        scratch_shapes=[pltpu.VMEM((tm + 2 * H, CONV_W), F32), pltpu.VMEM((tm, CONV_W), F32)],
        compiler_params=_cparams("parallel"),
    )(z, z, z, cwp, cb.reshape(1, -1), lng.reshape(1, -1), lnb.reshape(1, -1))


def _split3(x):
    a = x.astype(BF16)
    r = x - a.astype(F32)
    b = r.astype(BF16)
    c = (r - b.astype(F32)).astype(BF16)
    return a, b, c


def _hgrn_kernel(*refs, rows, nt):
    fwd, bwd = refs[0:5] + refs[10:12] + refs[14:15], refs[5:10] + refs[12:14] + refs[15:16]
    _hgrn_load_state(fwd[4], fwd[7], rows=rows, reverse=False, nt=nt)
    _hgrn_load_state(bwd[4], bwd[7], rows=rows, reverse=True, nt=nt)
    _hgrn_step(*fwd, reverse=False)
    _hgrn_step(*bwd, reverse=True)


def _hgrn_load_state(s0_ref, st_ref, *, rows, reverse, nt):
    t = pl.program_id(1)
    local, per = rows.seg_local((nt - 1 - t) if reverse else t, REC_TILE)

    @pl.when((local == per - 1) if reverse else (local == 0))
    def _():
        st_ref[...] = s0_ref[...].T


def _hgrn_step(zq_ref, zi_ref, zf_ref, lb_ref, s0_ref, o_ref, sf_ref, st_ref, *, reverse):
    T, SUB = REC_TILE, REC_SUB
    zq, v, lb = zq_ref[...], zi_ref[...], lb_ref[0]
    q = _silu(zq)
    f = lb + (1.0 - lb) * jax.nn.sigmoid(zf_ref[...])
    k = 1.0 - f
    g = jnp.log2(f)
    row = lax.broadcasted_iota(jnp.int32, (T, T), 0)
    col = lax.broadcasted_iota(jnp.int32, (T, T), 1)
    tri = jnp.where((col >= row) if reverse else (col <= row), 1.0, 0.0).astype(BF16)
    g1, g2, g3 = _split3(g)
    b = _dot(tri, g1) + _dot(tri, g2) + _dot(tri, g3)
    btot = jnp.sum(g, axis=0, keepdims=True)
    st = st_ref[...]
    vb = v.astype(BF16)

    o = _nt_dot((q * jnp.exp2(b)).astype(BF16), st.astype(BF16))

    rowv = lax.broadcasted_iota(jnp.int32, (T, 1), 0)
    sub_row = lax.broadcasted_iota(jnp.int32, (SUB, 1), 0)
    lane = lax.broadcasted_iota(jnp.int32, (1, T), 1)
    n_sub = T // SUB
    blocks = []
    for i in range(n_sub):
        lo, hi = SUB * i, SUB * (i + 1)
        has_off = (i < n_sub - 1) if reverse else (i > 0)
        if has_off:
            r = b[hi:hi + 1] if reverse else b[lo - 1:lo]
            qt = (q[lo:hi] * jnp.exp2(b[lo:hi] - r)).astype(BF16)
            kt = k * jnp.exp2(jnp.minimum(r - b, 0.0))
            kt = jnp.where((rowv >= hi) if reverse else (rowv < lo), kt, 0.0).astype(BF16)
            blocks.append(_nt_dot(qt, kt))
        else:
            blocks.append(jnp.zeros((SUB, T), F32))
    a = jnp.concatenate(blocks, axis=0)

    HALF = SUB // 2
    upper_half, lower_half = (rowv % SUB) >= HALF, (rowv % SUB) < HALF
    t_side, s_side = (lower_half, upper_half) if reverse else (upper_half, lower_half)
    mid = [b[SUB * i + HALF:SUB * i + HALF + 1] if reverse else b[SUB * i + HALF - 1:SUB * i + HALF]
           for i in range(n_sub)]
    r_mid = jnp.concatenate([jnp.broadcast_to(m, (SUB, T)) for m in mid], axis=0)
    qh = jnp.where(t_side, q * jnp.exp2(jnp.where(t_side, b - r_mid, 0.0)), 0.0).astype(BF16)
    kh = jnp.where(s_side, k * jnp.exp2(jnp.where(s_side, r_mid - b, 0.0)), 0.0).astype(BF16)
    same_block = (rowv // SUB) == (lane // SUB)
    a = a + jnp.where(same_block, _nt_dot(qh, kh), 0.0)

    half_row = lax.broadcasted_iota(jnp.int32, (HALF, 1), 0)
    diag = [a[HALF * j:HALF * (j + 1)] for j in range(T // HALF)]
    for s in range(HALF):
        for j in range(T // HALF):
            lo, hi = HALF * j, HALF * (j + 1)
            rs = lo + s
            e = jnp.exp2(b[lo:hi] - b[rs:rs + 1])
            cv = jnp.sum(q[lo:hi] * e * k[rs:rs + 1], axis=-1, keepdims=True)
            cv = jnp.where((half_row <= s) if reverse else (half_row >= s), cv, 0.0)
            diag[j] = jnp.where(lane == rs, cv, diag[j])
    a = jnp.concatenate(diag, axis=0)
    o_ref[...] = o + _dot(a.astype(BF16), vb)

    kd = (k * jnp.exp2(btot - b)).astype(BF16)
    st_new = st * jnp.exp2(btot) + _dot(v.T.astype(BF16), kd)
    st_ref[...] = st_new
    sf_ref[...] = st_new.T


def _hgrn_scan(z, lb16, s0, rows):
    R = z.shape[0]
    T = REC_TILE
    nt = R // T
    nseg = rows.nb_p + rows.nb_s

    def specs(d):
        tile = (lambda t: nt - 1 - t) if d else (lambda t: t)
        zblk = lambda off: pl.BlockSpec((T, LANES), lambda h, t: (tile(t), off // LANES + h))
        ins = [zblk(OFF_RQ), zblk(OFF_RI), zblk(OFF_FB if d else OFF_FF),
               pl.BlockSpec((1, 1, LANES), lambda h, t: (d * N_HEADS_R + h, 0, 0)),
               pl.BlockSpec((None, None, None, HEAD_DIM_R, HEAD_DIM_R),
                            lambda h, t: (d, rows.seg_index(tile(t), T), h, 0, 0))]
        outs = [pl.BlockSpec((T, LANES), lambda h, t: (tile(t), h)),
                pl.BlockSpec((None, None, HEAD_DIM_R, HEAD_DIM_R),
                             lambda h, t: (rows.seg_index(tile(t), T), h, 0, 0))]
        return ins, outs

    (in_f, out_f), (in_b, out_b) = specs(0), specs(1)
    shapes = [jax.ShapeDtypeStruct((R, REC_W), F32),
              jax.ShapeDtypeStruct((nseg, N_HEADS_R, HEAD_DIM_R, HEAD_DIM_R), F32)]
    o_f, s_f, o_b, s_b = pl.pallas_call(
        functools.partial(_hgrn_kernel, rows=rows, nt=nt),
        grid=(N_HEADS_R, nt),
        in_specs=in_f + in_b,
        out_specs=out_f + out_b,
        out_shape=shapes + shapes,
        scratch_shapes=[pltpu.VMEM((HEAD_DIM_R, HEAD_DIM_R), F32)] * 2,
        compiler_params=_cparams("parallel", "arbitrary"),
    )(z, z, z, lb16, s0, z, z, z, lb16, s0)
    return o_f, s_f, o_b, s_b


def _rec_post_kernel(of_ref, ob_ref, zg_ref, g_ref, o_ref):
    for h in range(N_HEADS_R):
        sl = slice(h * LANES, (h + 1) * LANES)
        o = of_ref[:, sl] + ob_ref[:, sl]
        y = o * lax.rsqrt(jnp.mean(o * o, axis=-1, keepdims=True) + EPS) * g_ref[...]
        o_ref[:, sl] = (y * _silu(zg_ref[:, sl])).astype(o_ref.dtype)


def _rec_post(o_f, o_b, z, g, rows):
    R = z.shape[0]
    tm = min(256, rows.seq_p)
    blk = pl.BlockSpec((tm, REC_W), lambda i: (i, 0))
    return pl.pallas_call(
        _rec_post_kernel,
        grid=(R // tm,),
        in_specs=[blk, blk, pl.BlockSpec((tm, REC_W), lambda i: (i, OFF_RG // REC_W)),
                  pl.BlockSpec((1, LANES), lambda i: (0, 0))],
        out_specs=blk,
        out_shape=jax.ShapeDtypeStruct((R, REC_W), BF16),
        compiler_params=_cparams("parallel"),
    )(o_f, o_b, z, g.reshape(1, LANES))


def _merge_kernel(uap_ref, uas_ref, uc_ref, uh_ref, wa_ref, wc_ref, wh_ref, ga_ref, gc_ref, gh_ref, o_ref, *, npt):
    ua = jnp.where(pl.program_id(1) < npt, uap_ref[...], uas_ref[...])
    m = jax.nn.sigmoid(ga_ref[...]) * _dot(ua, wa_ref[...])
    m = m + jax.nn.sigmoid(gc_ref[...]) * _dot(uc_ref[...], wc_ref[...])
    m = m + jax.nn.sigmoid(gh_ref[...]) * _dot(uh_ref[...], wh_ref[...])
    o_ref[...] = m.astype(o_ref.dtype)


def _merge(ua_p, ua_s, uc, uh, wa, wc, wh, z):
    R = uc.shape[0]
    D = wa.shape[1]
    tm, tn = min(512, ua_p.shape[0]), min(1024, D)
    npt = ua_p.shape[0] // tm
    gb = OFF_GATE // tn
    nd = D // tn
    act = lambda w: pl.BlockSpec((tm, w), lambda j, i: (i, 0))
    wgt = lambda w: pl.BlockSpec((w, tn), lambda j, i: (0, j))
    gate = lambda br: pl.BlockSpec((tm, tn), lambda j, i: (i, gb + br * nd + j))
    return pl.pallas_call(
        functools.partial(_merge_kernel, npt=npt),
        grid=(D // tn, R // tm),
        in_specs=[pl.BlockSpec((tm, ATTN_W), lambda j, i: (jnp.minimum(i, npt - 1), 0)),
                  pl.BlockSpec((tm, ATTN_W), lambda j, i: (jnp.maximum(i - npt, 0), 0)),
                  act(CONV_W), act(REC_W), wgt(ATTN_W), wgt(CONV_W), wgt(REC_W),
                  gate(0), gate(1), gate(2)],
        out_specs=pl.BlockSpec((tm, tn), lambda j, i: (i, j)),
        out_shape=jax.ShapeDtypeStruct((R, D), BF16),
        compiler_params=_cparams("parallel", "parallel"),
    )(ua_p, ua_s, uc, uh, wa, wc, wh, z, z, z)


def _router_kernel(w_ref, h_ref, o_ref):
    s = _nt_dot(w_ref[...], h_ref[...].astype(BF16))
    e = jnp.exp(s - jnp.max(s, axis=0, keepdims=True))
    o_ref[...] = e / jnp.sum(e, axis=0, keepdims=True)


def _router(h, w_router):
    R, D = h.shape
    tm = min(512, R)
    wt = w_router.T.astype(BF16)
    return pl.pallas_call(
        _router_kernel,
        grid=(R // tm,),
        in_specs=[pl.BlockSpec((N_EXPERTS, D), lambda i: (0, 0)),
                  pl.BlockSpec((tm, D), lambda i: (i, 0))],
        out_specs=pl.BlockSpec((N_EXPERTS, tm), lambda i: (0, i)),
        out_shape=jax.ShapeDtypeStruct((N_EXPERTS, R), F32),
        compiler_params=_cparams("parallel"),
    )(wt, h)


TOK_RADIX = 64


def _topk_kernel(a_ref, slot_ref, idx_ref, gs_ref, lhs_ref, pm_ref, *, n, cap, capp, blocks_per_route):
    x = a_ref[...]
    bits = pltpu.bitcast(x, jnp.int32)

    def search(i, thr):
        cand = thr | (1 << (30 - i))
        cnt = jnp.sum(jnp.where(bits >= cand, 1.0, 0.0), axis=-1, keepdims=True)
        return jnp.where(cnt >= cap, cand, thr)

    thr = lax.fori_loop(0, 31, search, jnp.zeros((N_EXPERTS, 1), jnp.int32))
    gt = bits > thr
    eq = bits == thr
    need = cap - jnp.sum(jnp.where(gt, 1.0, 0.0), axis=-1, keepdims=True)

    cb = min(n, 512)
    ci = lax.broadcasted_iota(jnp.int32, (cb, cb), 0)
    cj = lax.broadcasted_iota(jnp.int32, (cb, cb), 1)
    upper = jnp.where(ci <= cj, 1.0, 0.0).astype(BF16)

    def prefix(mask01):
        outs, run = [], jnp.zeros((N_EXPERTS, 1), F32)
        for c in range(n // cb):
            inc = _dot(mask01[:, c * cb:(c + 1) * cb].astype(BF16), upper) + run
            outs.append(inc)
            run = inc[:, cb - 1:cb]
        return jnp.concatenate(outs, axis=1) if len(outs) > 1 else outs[0]

    eq01 = jnp.where(eq, 1.0, 0.0)
    sel = gt | (eq & (prefix(eq01) - eq01 < need))
    sel01 = jnp.where(sel, 1.0, 0.0)
    pos = prefix(sel01) - sel01
    blk = pl.program_id(0) % blocks_per_route
    slot_ref[...] = jnp.where(sel, pos.astype(jnp.int32) + blk * cap, -1)

    tok = lax.broadcasted_iota(jnp.int32, (1, n), 1)
    gsel = jnp.where(sel, x, 0.0)
    g1 = gsel.astype(BF16).astype(F32)
    r1 = gsel - g1
    g2 = r1.astype(BF16).astype(F32)
    g3 = (r1 - g2).astype(BF16).astype(F32)
    pm_ref[...] = jnp.where(sel, pos, -1.0)
    lhs_ref[...] = jnp.zeros(lhs_ref.shape, F32)
    for e in range(N_EXPERTS):
        lhs_ref[e, 0:1, :] = (tok // TOK_RADIX).astype(F32)
        lhs_ref[e, 1:2, :] = (tok % TOK_RADIX).astype(F32)
        lhs_ref[e, 2:3, :] = g1[e:e + 1]
        lhs_ref[e, 3:4, :] = g2[e:e + 1]
        lhs_ref[e, 4:5, :] = g3[e:e + 1]
    p_iota = lax.broadcasted_iota(jnp.int32, (capp, 1), 0).astype(F32)

    def per_expert(e, carry):
        def per_chunk(c, res):
            cs = pl.ds(pl.multiple_of(c * cb, cb), cb)
            onehot = jnp.where(pm_ref[pl.ds(e, 1), cs] == p_iota, 1.0, 0.0).astype(BF16)
            return res + _nt_dot(lhs_ref[e, :, cs].astype(BF16), onehot)
        res = lax.fori_loop(0, n // cb, per_chunk, jnp.zeros((8, capp), F32))
        idx = (res[0:1] * TOK_RADIX + res[1:2]).astype(jnp.int32) + blk * n
        idx_ref[pl.ds(e, 1)] = idx.reshape(1, 1, capp)
        gs_ref[pl.ds(e, 1)] = (res[2:3] + res[3:4] + res[4:5]).reshape(1, 1, capp)
        return carry

    lax.fori_loop(0, N_EXPERTS, per_expert, 0)


def _topk(aff_t, n, cap, col0, ncols, blocks_per_route):
    nb = ncols // n
    off = col0 // n
    capp = -(-cap // LANES) * LANES
    blk_in = pl.BlockSpec((N_EXPERTS, n), lambda s: (0, off + s))
    blk_slot = pl.BlockSpec((None, N_EXPERTS, 1, capp), lambda s: (s, 0, 0, 0))
    return pl.pallas_call(
        functools.partial(_topk_kernel, n=n, cap=cap, capp=capp, blocks_per_route=blocks_per_route),
        grid=(nb,),
        in_specs=[blk_in],
        out_specs=[pl.BlockSpec((N_EXPERTS, n), lambda s: (0, s)), blk_slot, blk_slot],
        out_shape=[jax.ShapeDtypeStruct((N_EXPERTS, ncols), jnp.int32),
                   jax.ShapeDtypeStruct((nb, N_EXPERTS, 1, capp), jnp.int32),
                   jax.ShapeDtypeStruct((nb, N_EXPERTS, 1, capp), F32)],
        scratch_shapes=[pltpu.VMEM((N_EXPERTS, 8, n), F32), pltpu.VMEM((N_EXPERTS, n), F32)],
        compiler_params=_cparams("parallel"),
    )(aff_t)


def _gather_kernel(idx_ref, h_hbm, x_ref, buf_ref, sem, *, cap):
    s, e = pl.program_id(0), pl.program_id(1)
    r = s * N_EXPERTS + e

    def row_copy(p, src_row):
        return pltpu.make_async_copy(h_hbm.at[pl.ds(src_row, 1)], buf_ref.at[pl.ds(p, 1)], sem)

    def start(p, carry):
        row_copy(p, idx_ref[r, p]).start()
        return carry

    def wait(p, carry):
        row_copy(p, 0).wait()
        return carry

    lax.fori_loop(0, cap, start, 0, unroll=8)
    lax.fori_loop(0, cap, wait, 0, unroll=8)
    x_ref[0] = buf_ref[...].astype(x_ref.dtype)


def _gather(idx, h, nseg, cap):
    R, D = h.shape
    return pl.pallas_call(
        functools.partial(_gather_kernel, cap=cap),
        grid_spec=pltpu.PrefetchScalarGridSpec(
            num_scalar_prefetch=1,
            grid=(nseg, N_EXPERTS),
            in_specs=[pl.BlockSpec(memory_space=pl.ANY)],
            out_specs=pl.BlockSpec((1, cap, D), lambda s, e, idx: (e, s, 0)),
            scratch_shapes=[pltpu.VMEM((cap, D), F32), pltpu.SemaphoreType.DMA(())]),
        out_shape=jax.ShapeDtypeStruct((N_EXPERTS, nseg * cap, D), BF16),
        compiler_params=_cparams("arbitrary", "arbitrary"),
    )(idx, h)


def _ffn_up_kernel(x_ref, wg_ref, wu_ref, o_ref):
    x = x_ref[0]
    a = _dot(x, wg_ref[...].astype(BF16))
    b = _dot(x, wu_ref[...].astype(BF16))
    o_ref[0] = (_silu(a) * b).astype(o_ref.dtype)


def _ffn_up(x, wg, wu, l):
    E, M, D = x.shape
    Fh = wg.shape[3]
    tm = M // 2 if (M // 2) % 16 == 0 else M
    tf = min(256, Fh)
    wblk = pl.BlockSpec((None, None, D, tf), lambda e, i, j: (l, e, 0, j))
    return pl.pallas_call(
        _ffn_up_kernel,
        grid=(E, M // tm, Fh // tf),
        in_specs=[pl.BlockSpec((1, tm, D), lambda e, i, j: (e, i, 0)), wblk, wblk],
        out_specs=pl.BlockSpec((1, tm, tf), lambda e, i, j: (e, i, j)),
        out_shape=jax.ShapeDtypeStruct((E, M, Fh), BF16),
        compiler_params=_cparams("parallel", "parallel", "parallel"),
    )(x, wg, wu)


def _ffn_down_kernel(h_ref, w_ref, g_ref, o_ref):
    o_ref[0] = (_dot(h_ref[0], w_ref[...].astype(BF16)) * g_ref[0]).astype(o_ref.dtype)


def _ffn_down(hid, wd, gate_col, l):
    E, M, Fh = hid.shape
    D = wd.shape[3]
    tm = M // 2 if (M // 2) % 16 == 0 else M
    tn = min(1024, D)
    return pl.pallas_call(
        _ffn_down_kernel,
        grid=(E, M // tm, D // tn),
        in_specs=[pl.BlockSpec((1, tm, Fh), lambda e, i, j: (e, i, 0)),
                  pl.BlockSpec((None, None, Fh, tn), lambda e, i, j: (l, e, 0, j)),
                  pl.BlockSpec((1, tm, 1), lambda e, i, j: (e, i, 0))],
        out_specs=pl.BlockSpec((1, tm, tn), lambda e, i, j: (e, i, j)),
        out_shape=jax.ShapeDtypeStruct((E, M, D), BF16),
        compiler_params=_cparams("parallel", "parallel", "parallel"),
    )(hid, wd, gate_col)


COMBINE_TOKENS = 256
COMBINE_WINDOW = 64
ROW_ALIGN = 16


def _combine_kernel(win_ref, nr_ref, row_ref, y_hbm, x_ref, g_ref, o_ref, buf_ref, sem, *, m_rows):
    i, nt = pl.program_id(0), pl.num_programs(0)
    W = COMBINE_WINDOW
    slot = i % 2
    rows_te = row_ref[...]
    lane = lax.broadcasted_iota(jnp.int32, (1, LANES), 1)

    def bounds(tile, r, e):
        lo = win_ref[tile, e] + r * W
        return lo, jnp.minimum(lo, m_rows - W)

    def window_copy(sl, e, start):
        return pltpu.make_async_copy(y_hbm.at[e, pl.ds(start, W)], buf_ref.at[sl, pl.ds(e * W, W)], sem.at[sl])

    def start_round(tile, r, sl):
        for e in range(N_EXPERTS):
            window_copy(sl, e, pl.multiple_of(bounds(tile, r, e)[1], ROW_ALIGN)).start()

    @pl.when((i == 0) & (nr_ref[0] > 0))
    def _():
        start_round(0, 0, 0)

    nxt = jnp.minimum(i + 1, nt - 1)

    @pl.when((i + 1 < nt) & (nr_ref[nxt] > 0))
    def _():
        start_round(nxt, 0, 1 - slot)

    def one_round(r, acc):
        @pl.when(r > 0)
        def _():
            start_round(i, r, slot)

        for e in range(N_EXPERTS):
            window_copy(slot, e, 0).wait()
        groups = []
        for g in range(N_EXPERTS // 2):
            rel = []
            for e in (2 * g, 2 * g + 1):
                lo, start = bounds(i, r, e)
                col = rows_te[:, e:e + 1]
                rel.append(jnp.where(col >= lo, col - start, -1))
            target = jnp.where(lane < W, rel[0], rel[1] + W)
            groups.append(jnp.where(target == lane, 1.0, 0.0).astype(BF16))
        onehot = jnp.concatenate(groups, axis=1)
        return acc + _dot(onehot, buf_ref[slot])

    acc = lax.fori_loop(0, nr_ref[i], one_round, jnp.zeros(x_ref.shape, F32))
    o_ref[...] = x_ref[...] + g_ref[0] * acc


def _combine(slot, y, x, mod, rows, i_gate, n, cap):
    R, D = x.shape
    E, M = y.shape[0], y.shape[1]
    T, W = min(COMBINE_TOKENS, n), COMBINE_WINDOW
    nt, tps, nseg = R // T, n // T, R // n
    sel = slot >= 0
    row = jnp.where(sel, slot + (jnp.arange(R) // n * cap)[None, :], -1)
    cnt = jnp.sum(sel.reshape(E, nseg, tps, T), axis=-1)
    lo = jnp.cumsum(cnt, axis=-1) - cnt + (jnp.arange(nseg) * cap)[None, :, None]
    lo_al = lo // ROW_ALIGN * ROW_ALIGN
    rounds = jnp.where(cnt > 0, (lo + cnt - lo_al + W - 1) // W, 0)
    nr = jnp.max(rounds, axis=0).reshape(nt).astype(jnp.int32)
    win = lo_al.reshape(E, nt).T.astype(jnp.int32)
    tok = lambda w: pl.BlockSpec((T, w), lambda i, *_: (i, 0))
    return pl.pallas_call(
        functools.partial(_combine_kernel, m_rows=M),
        grid_spec=pltpu.PrefetchScalarGridSpec(
            num_scalar_prefetch=2,
            grid=(nt,),
            in_specs=[tok(E), pl.BlockSpec(memory_space=pl.ANY), tok(D),
                      pl.BlockSpec((1, 1, D), lambda i, *_: (rows.mod_row(i, T) * N_MOD + i_gate, 0, 0))],
            out_specs=tok(D),
            scratch_shapes=[pltpu.VMEM((2, E * W, D), BF16), pltpu.SemaphoreType.DMA((2,))]),
        out_shape=jax.ShapeDtypeStruct((R, D), F32),
        compiler_params=_cparams("arbitrary"),
    )(win, nr, row.T.astype(jnp.int32), y, x, mod)


def _moe(h2, x, mod, rows, l, w_router, w_gate, w_up, w_down):
    P, S, E = rows.P, rows.seq_s, N_EXPERTS
    nseg = rows.R // S
    aff_t = _router(h2, w_router[l])
    cap_p = CAP_FACTOR * rows.seq_p // E
    cap_s = CAP_FACTOR * S // E
    slot_p, idx_p, gs_p = _topk(aff_t, rows.seq_p, cap_p, 0, P, rows.nb_p)
    slot_s, idx_s, gs_s = _topk(aff_t, S, cap_s, P, rows.R - P, 1)

    def per_block(a_p, a_s):
        a_p = a_p[:, :, 0, :cap_p].transpose(1, 0, 2).reshape(1, E, rows.nb_p * cap_p)
        return jnp.concatenate([a_p, a_s[:, :, 0, :cap_s]], axis=0)

    idx = per_block(idx_p, idx_s) + (jnp.arange(nseg, dtype=jnp.int32) * S)[:, None, None]
    gate_col = per_block(gs_p, gs_s).transpose(1, 0, 2).reshape(E, nseg * cap_s, 1)
    xs = _gather(idx.reshape(nseg * E, cap_s), h2, nseg, cap_s)
    ys = _ffn_down(_ffn_up(xs, w_gate, w_up, l), w_down, gate_col, l)
    slot = jnp.concatenate([slot_p, slot_s], axis=1)
    return _combine(slot, ys, x, mod, rows, 5, S, cap_s)


def _layer(x, l, rows, mod, p, tables, lam_val, lam_init, lbs, cache_k, cache_v, state_hgrn):
    R, D = x.shape
    P = rows.P
    h = _modulate(x, p["norm1_g"][l], mod, rows, 0, 1, BF16)
    z = _linear(h, p["w_in"], l, 1024, 512, F32)

    gq, gk = p["qk_norm_g"][l, 0], p["qk_norm_g"][l, 1]
    tile2 = lambda g: jnp.tile(g, 2).reshape(1, LANES)
    offset = 1.02 * HEAD_DIM_A * Q_SCALE * jnp.max(jnp.abs(gq)) * jnp.max(jnp.abs(gk))
    pad = jnp.zeros((8, LANES), F32).at[0, HEAD_DIM_A].set(-offset).at[1, HEAD_DIM_A].set(1.0)
    qa, qb, ka, kb, k_f, vx = _qkv_prep(z, tile2(gq), tile2(gk), pad, tables[0], tables[1], rows)
    prm = jnp.zeros((8, LANES), F32)
    prm = prm.at[0].set(p["attn_norm_g"][l]).at[1].set(lam_val).at[2].set(1.0 - lam_init)
    fast_ok = offset <= FAST_SOFTMAX_MAX_OFFSET
    ck = cache_k[:, l]
    kpad = jnp.broadcast_to(pad[1, HEAD_DIM_A:], ck.shape[:3] + (HEAD_DIM_A,))
    ctx_k = [jnp.concatenate([ck[:, :, :, m], kpad], axis=-1).reshape(rows.nb_s, -1, ATTN_W).astype(BF16)
             for m in range(2)]
    cv = cache_v[:, l]
    ctx_v = jnp.concatenate([cv, jnp.ones_like(cv)], axis=-1).reshape(rows.nb_s, -1, 2 * ATTN_W).astype(BF16)
    ua_p = _attention(prm, qa, qb, ka, kb, vx, None, 0, rows.nb_p, rows.seq_p, 256, 256, fast_ok)
    ua_s = _attention(prm, qa, qb, ka, kb, vx, (ctx_k[0], ctx_k[1], ctx_v), P, rows.nb_s, rows.seq_s,
                      2048, 512, fast_ok)

    uc = _conv_branch(z, p["conv_w"][l], p["conv_b"][l], p["conv_ln_g"][l], p["conv_ln_b"][l], rows)

    s0 = jnp.concatenate([jnp.zeros((2, rows.nb_p, N_HEADS_R, HEAD_DIM_R, HEAD_DIM_R), F32),
                          jnp.swapaxes(state_hgrn[:, l], 0, 1).astype(F32)], axis=1)
    lb16 = lbs[l].reshape(2 * N_HEADS_R, 1, HEAD_DIM_R)
    o_f, s_f, o_b, s_b = _hgrn_scan(z, lb16, s0, rows)
    uh = _rec_post(o_f, o_b, z, p["hgrn_norm_g"][l], rows)

    bf = lambda w: w[l].astype(BF16)
    m = _merge(ua_p, ua_s, uc, uh, bf(p["w_attn_o"]), bf(p["w_conv_o"]), bf(p["w_hgrn_o"]), z)
    x = _linear_res(m, bf(p["w_out"]), x, mod, rows, 2, 512, 1024)

    h2 = _modulate(x, p["norm2_g"][l], mod, rows, 3, 4, F32)
    x = _moe(h2, x, mod, rows, l, p["w_router"], p["w_e_gate"], p["w_e_up"], p["w_e_down"])

    new_k = k_f[:P].reshape(rows.nb_p, rows.seq_p, N_HEADS_A, 2, HEAD_DIM_A)
    new_v = z[:P, OFF_V:OFF_V + ATTN_W].reshape(rows.nb_p, rows.seq_p, N_HEADS_A, 2 * HEAD_DIM_A)
    new_s = jnp.stack([s_f[:rows.nb_p], s_b[:rows.nb_p]], axis=1)
    return x, new_k, new_v, new_s


def _rope_tables(n_tok, tm):
    t = jnp.arange(n_tok)
    row = (t // GRID_W).astype(F32)
    col = (t % GRID_W).astype(F32)
    inv = ROPE_BASE ** (-jnp.arange(0, AXIS_DIM, 2, dtype=F32) / AXIS_DIM)
    ar, ac = row[:, None] * inv, col[:, None] * inv
    ang = jnp.concatenate([ar, ar, ac, ac], axis=-1)
    sign = jnp.where((jnp.arange(HEAD_DIM_A) & (AXIS_DIM // 2)) != 0, 1.0, -1.0).astype(F32)
    cos = jnp.tile(jnp.cos(ang), (1, 2))
    sin = jnp.tile(jnp.sin(ang) * sign, (1, 2))
    cos = jnp.concatenate([jnp.ones((tm, LANES), F32), cos], axis=0)
    sin = jnp.concatenate([jnp.zeros((tm, LANES), F32), sin], axis=0)
    return cos, sin


def kernel(x_prompt, x_sample, cache_k, cache_v, state_hgrn, c, c_ctx, norm1_g, norm2_g, w_ada, b_ada, w_in, qk_norm_g, lam, attn_norm_g, w_attn_o, conv_w, conv_b, conv_ln_g, conv_ln_b, w_conv_o, hgrn_lb, hgrn_norm_g, w_hgrn_o, w_out, w_router, w_e_gate, w_e_up, w_e_down):
    nb_p, seq_p, D = x_prompt.shape
    nb_s, seq_s, _ = x_sample.shape
    depth = w_in.shape[0]
    rows = _Rows(nb_p, seq_p, nb_s, seq_s)
    p = dict(norm1_g=norm1_g, norm2_g=norm2_g, w_in=w_in, qk_norm_g=qk_norm_g, attn_norm_g=attn_norm_g,
             w_attn_o=w_attn_o, conv_w=conv_w, conv_b=conv_b, conv_ln_g=conv_ln_g, conv_ln_b=conv_ln_b,
             w_conv_o=w_conv_o, hgrn_norm_g=hgrn_norm_g, w_hgrn_o=w_hgrn_o, w_out=w_out, w_router=w_router,
             w_e_gate=w_e_gate, w_e_up=w_e_up, w_e_down=w_e_down)

    sm = jax.nn.softmax(hgrn_lb.astype(F32), axis=0)
    lbs = jnp.cumsum(sm, axis=0) - sm[0]
    tables = _rope_tables(seq_s, min(256, seq_p))
    cond8 = jnp.concatenate([c_ctx[None], c, jnp.zeros((8 - 1 - nb_s, D), F32)], axis=0)

    x = jnp.concatenate([x_prompt.reshape(rows.P, D), x_sample.reshape(nb_s * seq_s, D)], axis=0)
    new_k, new_v, new_s = [], [], []
    for l in range(depth):
        lam_init = 0.8 - 0.6 * math.exp(-0.3 * l)
        lam_l = lam[l].astype(F32)
        lam_val = jnp.exp(jnp.sum(lam_l[0] * lam_l[1])) - jnp.exp(jnp.sum(lam_l[2] * lam_l[3])) + lam_init
        mod = _ada(cond8, w_ada, b_ada, l).reshape(8 * N_MOD, 1, D)
        x, k_l, v_l, s_l = _layer(x, l, rows, mod, p, tables, lam_val, lam_init, lbs, cache_k, cache_v, state_hgrn)
        new_k.append(k_l)
        new_v.append(v_l)
        new_s.append(s_l)
    y_prompt = x[:rows.P].reshape(nb_p, seq_p, D)
    y_sample = x[rows.P:].reshape(nb_s, seq_s, D)
    return (y_prompt, y_sample, jnp.stack(new_k, axis=1), jnp.stack(new_v, axis=1), jnp.stack(new_s, axis=1))
```
